```python
import jax
import jax.numpy as jnp
from jax import lax
import numpy as np

D_MODEL = 1024
BATCH = 4
SEQ = 4096
DEPTH = 4

MIX_WIDTH = D_MODEL
POOL_WINDOWS = (2, 4, 8, 16)
POOL_WIDTH = MIX_WIDTH // 2
POOL_GROUP = POOL_WIDTH // len(POOL_WINDOWS)
HGRN_WIDTH = MIX_WIDTH - POOL_WIDTH
HGRN_EXPAND = 128
HGRN_HEADS = HGRN_WIDTH // HGRN_EXPAND
HGRN_HEAD_I = HGRN_WIDTH // HGRN_HEADS
HGRN_FDIM = HGRN_HEADS * HGRN_EXPAND
HGRN_CHUNK = 64
EVEN_IN = POOL_WIDTH + 2 * HGRN_FDIM + HGRN_WIDTH + MIX_WIDTH
RWKV_HEAD = 64
RWKV_HEADS = MIX_WIDTH // RWKV_HEAD
DECAY_LORA = 64
ICL_LORA = 64
VALUE_LORA = 32
N_EVEN = (DEPTH + 1) // 2
N_ODD = DEPTH // 2
NORM_EPS = 1e-6
LNX_EPS = 64e-5
F32 = jnp.float32

kernel_name = "hybrid_pool_hgrn2_rwkv7_adaln"


def rms_norm(x, g):
    xf = x.astype(F32)
    return xf * lax.rsqrt(jnp.mean(xf * xf, axis=-1, keepdims=True) + NORM_EPS) * g.astype(F32)


def token_shift(x):
    return jnp.pad(x, ((0, 0), (1, 0), (0, 0)))[:, :-1]


def causal_pool_mixer(u, pool_w, pool_scale):
    b, t, _ = u.shape
    count = jnp.arange(1, t + 1, dtype=F32)
    outs = []
    for gi, win in enumerate(POOL_WINDOWS):
        ug = u[..., gi * POOL_GROUP:(gi + 1) * POOL_GROUP]
        cs = jnp.cumsum(ug, axis=1)
        lagged = jnp.pad(cs, ((0, 0), (win, 0), (0, 0)))[:, :t]
        mean = (cs - lagged) / jnp.minimum(count, float(win))[None, :, None]
        outs.append(mean - ug)
    p = jnp.stack(outs, axis=2)
    y = jnp.einsum("btgc,gcd->btgd", p, pool_w.astype(F32)).reshape(b, t, POOL_WIDTH)
    return y * pool_scale.astype(F32)


def to_chunks(z):
    b, t, h, d = z.shape
    return z.reshape(b, t // HGRN_CHUNK, HGRN_CHUNK, h, d).transpose(1, 0, 3, 2, 4)


def hgrn2_chunk_scan(q, k, v, logf):
    b, t, h, dk = q.shape
    dv = v.shape[-1]
    mask = jnp.tril(jnp.ones((HGRN_CHUNK, HGRN_CHUNK), dtype=bool))[None, None, :, :, None]

    def step(state, inp):
        qc, kc, vc, gc = inp
        cum = jnp.cumsum(gc, axis=2)
        o_inter = jnp.einsum("bhtd,bhde->bhte", qc * jnp.exp(cum), state)
        diff = cum[:, :, :, None, :] - cum[:, :, None, :, :]
        decay = jnp.where(mask, jnp.exp(jnp.where(mask, diff, 0.0)), 0.0)
        scores = jnp.einsum("bhtd,bhsd,bhtsd->bhts", qc, kc, decay)
        o_intra = jnp.einsum("bhts,bhse->bhte", scores, vc)
        last = cum[:, :, -1:, :]
        state = jnp.exp(last[:, :, 0, :])[..., None] * state + jnp.einsum(
            "bhsd,bhse->bhde", kc * jnp.exp(last - cum), vc)
        return state, o_inter + o_intra

    s0 = jnp.zeros((b, h, dk, dv), F32)
    _, o = lax.scan(step, s0, (to_chunks(q), to_chunks(k), to_chunks(v), to_chunks(logf)))
    return o.transpose(1, 0, 3, 2, 4).reshape(b, t, h, dv)


def hgrn2_mixer(q_raw, f_raw, i_raw, lb, o_norm_g):
    b, t, _ = q_raw.shape
    shp = (b, t, HGRN_HEADS, HGRN_EXPAND)
    q = jax.nn.silu(q_raw).reshape(shp)
    u = f_raw.reshape(shp)
    lb = lb.reshape(HGRN_HEADS, HGRN_EXPAND)
    logf = jnp.logaddexp(jnp.log(lb), jnp.log1p(-lb) + jax.nn.log_sigmoid(u))
    k = (1.0 - lb) * jax.nn.sigmoid(-u)
    v = i_raw.reshape(b, t, HGRN_HEADS, HGRN_HEAD_I)
    o = hgrn2_chunk_scan(q, k, v, logf)
    o = o * lax.rsqrt(jnp.mean(o * o, axis=-1, keepdims=True) + NORM_EPS) * o_norm_g.astype(F32).reshape(
        HGRN_HEADS, HGRN_HEAD_I)
    return o.reshape(b, t, HGRN_WIDTH)


def rwkv7_scan(r, w, k, v, kk, a):
    b, t, h, n = r.shape

    def step(state, inp):
        rt, wt, kt, vt, kkt, at = inp
        sa = jnp.einsum("bhvk,bhk->bhv", state, -kkt)
        state = (state * wt[:, :, None, :] + sa[..., None] * (kkt * at)[:, :, None, :]
                 + vt[..., None] * kt[:, :, None, :])
        return state, jnp.einsum("bhvk,bhk->bhv", state, rt)

    xs = tuple(jnp.moveaxis(z, 1, 0) for z in (r, w, k, v, kk, a))
    _, y = lax.scan(step, jnp.zeros((b, h, n, n), F32), xs)
    return jnp.moveaxis(y, 0, 1)


def rwkv7_mixer(h, mu, w_rkvz, w0, w1, w2, a0, a1, a2, k_k, k_a, r_k, lnx_g, lnx_b,
                v_first, v0, v1, v2):
    b, t, _ = h.shape
    xx = token_shift(h) - h
    xm = h[None] + xx[None] * mu.astype(F32)[:, None, None, :]
    rkvz = jnp.einsum("pbtd,pde->pbte", xm[:4], w_rkvz)
    r, k, v, z = rkvz[0], rkvz[1], rkvz[2], rkvz[3]
    logw = -jax.nn.softplus(-(w0 + jnp.tanh(xm[4] @ w1) @ w2)) - 0.5
    decay = jnp.exp(-jnp.exp(logw))
    a = jax.nn.sigmoid(a0 + (xm[5] @ a1) @ a2)
    if v_first is None:
        v_first = v
    else:
        v = v + (v_first - v) * jax.nn.sigmoid(v0 + (xm[2] @ v1) @ v2)
    heads = (b, t, RWKV_HEADS, RWKV_HEAD)
    kk = (k * k_k).reshape(heads)
    kk = kk / jnp.maximum(jnp.sqrt(jnp.sum(kk * kk, axis=-1, keepdims=True)), 1e-12)
    k = k * (1.0 + (a - 1.0) * k_a)
    r, k, v, decay, a = (z_.astype(F32).reshape(heads) for z_ in (r, k, v, decay, a))
    y = rwkv7_scan(r, decay, k, v, kk, a)
    mean = jnp.mean(y, axis=-1, keepdims=True)
    var = jnp.mean(jnp.square(y - mean), axis=-1, keepdims=True)
    y = ((y - mean) * lax.rsqrt(var + LNX_EPS)).reshape(b, t, MIX_WIDTH) * lnx_g + lnx_b
    bonus = jnp.sum(r * k * r_k.astype(F32), axis=-1, keepdims=True) * v
    y = y + bonus.reshape(b, t, MIX_WIDTH)
    return y, z, v_first


def setup_inputs(seed: int = 0) -> dict:
    key = jax.random.key(seed)
    ks = iter(jax.random.split(key, 32))

    def nrm(shape, scale):
        return scale * jax.random.normal(next(ks), shape, F32)

    D, W = D_MODEL, MIX_WIDTH
    return {
        "x": nrm((BATCH, SEQ, D), 1.0),
        "c": nrm((BATCH, D), 1.0),
        "ada_w": nrm((DEPTH, D, 3 * D), D ** -0.5),
        "ada_b": nrm((DEPTH, 3 * D), 0.01),
        "pre_g": 1.0 + nrm((DEPTH, D), 0.05),
        "post_g": 1.0 + nrm((DEPTH, D), 0.05),
        "ev_w_in": nrm((N_EVEN, D, EVEN_IN), D ** -0.5),
        "ev_w_out": nrm((N_EVEN, W, D), W ** -0.5),
        "pool_w": nrm((N_EVEN, len(POOL_WINDOWS), POOL_GROUP, POOL_GROUP), POOL_GROUP ** -0.5),
        "pool_scale": 1.0 + nrm((N_EVEN, POOL_WIDTH), 0.1),
        "hgrn_lb_logits": nrm((N_EVEN, HGRN_FDIM), 0.5),
        "hgrn_onorm_g": 1.0 + nrm((N_EVEN, HGRN_WIDTH), 0.05),
        "rw_mu": jax.random.uniform(next(ks), (N_ODD, 6, D), F32),
        "rw_w_rkvz": nrm((N_ODD, 4, D, W), D ** -0.5),
        "rw_w0": jax.random.uniform(next(ks), (N_ODD, W), F32, -5.0, 0.0),
        "rw_w1": nrm((N_ODD, D, DECAY_LORA), D ** -0.5),
        "rw_w2": nrm((N_ODD, DECAY_LORA, W), 0.1 * DECAY_LORA ** -0.5),
        "rw_a0": nrm((N_ODD, W), 0.1),
        "rw_a1": nrm((N_ODD, D, ICL_LORA), D ** -0.5),
        "rw_a2": nrm((N_ODD, ICL_LORA, W), 0.1 * ICL_LORA ** -0.5),
        "rw_k_k": 0.85 + nrm((N_ODD, W), 0.05),
        "rw_k_a": 1.0 + nrm((N_ODD, W), 0.05),
        "rw_r_k": nrm((N_ODD, RWKV_HEADS, RWKV_HEAD), 0.1),
        "rw_lnx_g": 1.0 + nrm((N_ODD, W), 0.05),
        "rw_lnx_b": nrm((N_ODD, W), 0.01),
        "rw_w_out": nrm((N_ODD, W, D), W ** -0.5),
        "rw_v0": nrm((N_ODD - 1, W), 0.1),
        "rw_v1": nrm((N_ODD - 1, D, VALUE_LORA), D ** -0.5),
        "rw_v2": nrm((N_ODD - 1, VALUE_LORA, W), 0.1 * VALUE_LORA ** -0.5),
    }


def reference(x, c, ada_w, ada_b, pre_g, post_g, ev_w_in, ev_w_out, pool_w, pool_scale,
              hgrn_lb_logits, hgrn_onorm_g, rw_mu, rw_w_rkvz, rw_w0, rw_w1, rw_w2, rw_a0, rw_a1,
              rw_a2, rw_k_k, rw_k_a, rw_r_k, rw_lnx_g, rw_lnx_b, rw_w_out, rw_v0, rw_v1, rw_v2):
    out_dtype = x.dtype
    res = x.astype(F32)
    cond = jax.nn.silu(c.astype(F32))
    lb_all = jnp.cumsum(jax.nn.softmax(hgrn_lb_logits.astype(F32), axis=0), axis=0)
    lb_all = lb_all - lb_all[0:1]
    o1 = POOL_WIDTH
    o2 = o1 + HGRN_FDIM
    o3 = o2 + HGRN_FDIM
    o4 = o3 + HGRN_WIDTH
    v_first = None
    for layer in range(DEPTH):
        j = layer // 2
        ada = cond @ ada_w[layer] + ada_b[layer]
        shift, scale, gate = jnp.split(ada, 3, axis=-1)
        h = rms_norm(res, pre_g[layer]) * (1.0 + scale[:, None, :]) + shift[:, None, :]
        if layer % 2 == 0:
            u = h @ ev_w_in[j]
            y_pool = causal_pool_mixer(u[..., :o1], pool_w[j], pool_scale[j])
            y_hgrn = hgrn2_mixer(u[..., o1:o2], u[..., o2:o3], u[..., o3:o4], lb_all[j], hgrn_onorm_g[j])
            y = jnp.concatenate([y_pool, y_hgrn], axis=-1) * jax.nn.silu(u[..., o4:])
            out = y @ ev_w_out[j]
        else:
            if v_first is None:
                y, z, v_first = rwkv7_mixer(h, rw_mu[j], rw_w_rkvz[j], rw_w0[j], rw_w1[j], rw_w2[j],
                                            rw_a0[j], rw_a1[j], rw_a2[j], rw_k_k[j], rw_k_a[j], rw_r_k[j],
                                            rw_lnx_g[j], rw_lnx_b[j], None, None, None, None)
            else:
                y, z, _ = rwkv7_mixer(h, rw_mu[j], rw_w_rkvz[j], rw_w0[j], rw_w1[j], rw_w2[j],
                                      rw_a0[j], rw_a1[j], rw_a2[j], rw_k_k[j], rw_k_a[j], rw_r_k[j],
                                      rw_lnx_g[j], rw_lnx_b[j], v_first, rw_v0[j - 1], rw_v1[j - 1],
                                      rw_v2[j - 1])
            out = (y * jax.nn.silu(z)) @ rw_w_out[j]
        res = res + gate[:, None, :] * rms_norm(out, post_g[layer])
    return res.astype(out_dtype)
```

```python
import functools

import jax
import jax.numpy as jnp
from jax import lax
from jax.experimental import pallas as pl
from jax.experimental.pallas import tpu as pltpu

F32 = jnp.float32
BF16 = jnp.bfloat16

NORM_EPS = 1e-6
LNX_EPS = 64e-5
POOL_WINDOWS = (2, 4, 8, 16)
MAX_WINDOW = 16
HGRN_HEAD = 128
RWKV_HEAD = 64
RWKV_CHUNK = 64
RWKV_SUB = 16
GROUP_LANES = 256
ROW_TILE = 256
VMEM_LIMIT_BYTES = 56 * 1024 * 1024


def _dot(a, b):
    return jnp.dot(a.astype(BF16), b.astype(BF16), preferred_element_type=F32)


def _dot_tb(a, b):
    return lax.dot_general(a.astype(BF16), b.astype(BF16), (((1,), (1,)), ((), ())),
                           preferred_element_type=F32)


def _dot_ta(a, b):
    return lax.dot_general(a.astype(BF16), b.astype(BF16), (((0,), (0,)), ((), ())),
                           preferred_element_type=F32)


def _split3(x):
    hi = x.astype(BF16)
    r1 = x - hi.astype(F32)
    mid = r1.astype(BF16)
    lo = (r1 - mid.astype(F32)).astype(BF16)
    return hi, mid, lo


def _dot_sel(sel, x):
    hi, mid, lo = _split3(x)
    f = lambda p: jnp.dot(sel, p, preferred_element_type=F32)
    return (f(lo) + f(mid)) + f(hi)


def _dot_sel_r(x, sel):
    hi = x.astype(BF16)
    lo = (x - hi.astype(F32)).astype(BF16)
    f = lambda p: jnp.dot(p, sel, preferred_element_type=F32)
    return f(lo) + f(hi)


def _sigmoid(x):
    return 1.0 / (1.0 + jnp.exp(-x))


def _silu(x):
    return x * _sigmoid(x)


def _softplus(x):
    return jnp.maximum(x, 0.0) + jnp.log1p(jnp.exp(-jnp.abs(x)))


def _rms(x, g):
    return x * lax.rsqrt(jnp.mean(x * x, axis=-1, keepdims=True) + NORM_EPS) * g


def _iota(shape, dim):
    return lax.broadcasted_iota(jnp.int32, shape, dim)


def _full_spec(shape):
    nd = len(shape)
    return pl.BlockSpec(shape, lambda *_: (0,) * nd)


def _params():
    return pltpu.CompilerParams(dimension_semantics=("arbitrary", "arbitrary"),
                                vmem_limit_bytes=VMEM_LIMIT_BYTES)


def _ada_kernel(c_ref, w_ref, b_ref, o_ref):
    cond = _silu(c_ref[...].astype(F32))
    o_ref[0] = jnp.dot(cond, w_ref[0].astype(F32), preferred_element_type=F32,
                       precision=lax.Precision.HIGHEST) + b_ref[0]


def _ada_all(c, ada_w, ada_b):
    depth, d, d3 = ada_w.shape
    bsz = c.shape[0]
    tn = 1024
    return pl.pallas_call(
        _ada_kernel,
        grid=(depth, d3 // tn),
        in_specs=[pl.BlockSpec((bsz, d), lambda l, n: (0, 0)),
                  pl.BlockSpec((1, d, tn), lambda l, n: (l, 0, n)),
                  pl.BlockSpec((1, 1, tn), lambda l, n: (l, 0, n))],
        out_specs=pl.BlockSpec((1, bsz, tn), lambda l, n: (l, 0, n)),
        out_shape=jax.ShapeDtypeStruct((depth, bsz, d3), F32),
        compiler_params=_params(),
        name="ada_ln",
    )(c, ada_w, ada_b.reshape(depth, 1, d3))


def _lb_kernel(logit_ref, o_ref):
    x = logit_ref[...].astype(F32)
    n = x.shape[0]
    m = jnp.max(x, axis=0, keepdims=True)
    e = jnp.exp(x - m)
    sm = e / jnp.sum(e, axis=0, keepdims=True)
    acc = jnp.zeros_like(sm[0:1])
    for i in range(n):
        if i > 0:
            acc = acc + sm[i:i + 1]
        o_ref[i:i + 1, :] = acc


def _lower_bounds(logits):
    return pl.pallas_call(
        _lb_kernel,
        out_shape=jax.ShapeDtypeStruct(logits.shape, F32),
        name="hgrn_lower_bounds",
    )(logits)


def _even_kernel(res_ref, ada_ref, preg_ref, postg_ref, win_ref, wout_ref, poolw_ref, pscale_ref,
                 lb_ref, og_ref, out_ref,
                 pbuf, kbuf, cbuf, vbuf, st_ref, qt_ref, kv_ref, dl_ref, oin_ref):
    ti = pl.program_id(1)
    tt, d = res_ref.shape[1], res_ref.shape[2]
    pw = pbuf.shape[1]
    nh = pw // HGRN_HEAD
    sub = MAX_WINDOW
    pad = MAX_WINDOW

    @pl.when(ti == 0)
    def _():
        zpad = jnp.zeros((pad, pw), F32)
        pbuf[0:pad, :] = zpad
        kbuf[0:pad, :] = zpad
        cbuf[0:pad, :] = zpad
        vbuf[0:pad, :] = zpad
        st_ref[...] = jnp.zeros(st_ref.shape, F32)

    x = res_ref[0]
    ada = ada_ref[0]
    shift, scale, gate = ada[:, :d], ada[:, d:2 * d], ada[:, 2 * d:]
    h = _rms(x, preg_ref[...]) * (1.0 + scale) + shift
    u = _dot(h, win_ref[...])

    row = _iota((tt, 1), 0)
    up = u[:, :pw]
    pbuf[pad:pad + tt, :] = up
    pos1 = (ti * tt + row + 1).astype(F32)
    gw = pw // len(POOL_WINDOWS)
    pooled = []
    for gi, win in enumerate(POOL_WINDOWS):
        sl = slice(gi * gw, (gi + 1) * gw)
        s = up[:, sl]
        for i in range(1, win):
            s = s + pbuf[pad - i:pad - i + tt, sl]
        p = s / jnp.minimum(pos1, float(win)) - up[:, sl]
        pooled.append(_dot(p, poolw_ref[gi]))
    pbuf[0:pad, :] = pbuf[tt:tt + pad, :]
    y_pool = jnp.concatenate(pooled, axis=-1) * pscale_ref[...]

    q = _silu(u[:, pw:2 * pw])
    fr = u[:, 2 * pw:3 * pw]
    v = u[:, 3 * pw:4 * pw]
    z = u[:, 4 * pw:]
    lb = lb_ref[...]
    k = (1.0 - lb) * _sigmoid(-fr)
    la = jnp.log(lb)
    lsig = jnp.minimum(fr, 0.0) - jnp.log1p(jnp.exp(-jnp.abs(fr)))
    lbb = jnp.log1p(-lb) + lsig
    logf = jnp.maximum(la, lbb) + jnp.log1p(jnp.exp(-jnp.abs(la - lbb)))

    r2 = _iota((tt, tt), 0)
    c2 = _iota((tt, tt), 1)
    same = (r2 // sub) == (c2 // sub)
    tri = jnp.where(same & (c2 <= r2), 1.0, 0.0).astype(BF16)
    blk = jnp.where(same, 1.0, 0.0).astype(BF16)
    cs = _dot_sel(tri, logf)
    csl = _dot_sel(blk, logf)
    qt_ref[...] = q * jnp.exp(cs)
    kv_ref[...] = k * jnp.exp(csl - cs)
    dl_ref[...] = jnp.exp(csl)
    kbuf[pad:pad + tt, :] = k
    cbuf[pad:pad + tt, :] = cs
    vbuf[pad:pad + tt, :] = v

    def sub_body(m, carry):
        r0 = pl.multiple_of(m * sub, sub)
        qtm = qt_ref[pl.ds(r0, sub), :]
        kvm = kv_ref[pl.ds(r0, sub), :]
        vm = vbuf[pl.ds(pad + r0, sub), :]
        dl = dl_ref[pl.ds(r0, 1), :]
        for hh in range(nh):
            sl = slice(hh * HGRN_HEAD, (hh + 1) * HGRN_HEAD)
            st = st_ref[hh]
            oin_ref[pl.ds(r0, sub), sl] = _dot_tb(qtm[:, sl], st)
            st_ref[hh] = st * dl[:, sl] + _dot_ta(vm[:, sl], kvm[:, sl])
        return carry

    lax.fori_loop(0, tt // sub, sub_body, 0)

    tloc = row % sub
    o_heads = [oin_ref[:, hh * HGRN_HEAD:(hh + 1) * HGRN_HEAD] for hh in range(nh)]
    for dlt in range(sub):
        valid = tloc >= dlt
        ks = kbuf[pad - dlt:pad - dlt + tt, :]
        css = cbuf[pad - dlt:pad - dlt + tt, :]
        vs = vbuf[pad - dlt:pad - dlt + tt, :]
        p = q * ks * jnp.exp(jnp.where(valid, cs - css, 0.0))
        for hh in range(nh):
            sl = slice(hh * HGRN_HEAD, (hh + 1) * HGRN_HEAD)
            sc = jnp.where(valid, jnp.sum(p[:, sl], axis=-1, keepdims=True), 0.0)
            o_heads[hh] = o_heads[hh] + sc * vs[:, sl]
    og = og_ref[...]
    y_h = [o * lax.rsqrt(jnp.mean(o * o, axis=-1, keepdims=True) + NORM_EPS)
           * og[:, hh * HGRN_HEAD:(hh + 1) * HGRN_HEAD] for hh, o in enumerate(o_heads)]

    y = jnp.concatenate([y_pool] + y_h, axis=-1) * _silu(z)
    out = _dot(y, wout_ref[...])
    out_ref[0] = x + gate * _rms(out, postg_ref[...])


def _even_layer(res, ada, pre_g, post_g, w_in, w_out, pool_w, pool_scale, lb, onorm_g):
    bsz, t, d = res.shape
    tt = min(ROW_TILE, t)
    pw = pool_scale.shape[-1]
    nh = pw // HGRN_HEAD
    row_spec = pl.BlockSpec((1, tt, d), lambda b, i: (b, i, 0))
    return pl.pallas_call(
        _even_kernel,
        grid=(bsz, t // tt),
        in_specs=[row_spec,
                  pl.BlockSpec((1, 1, 3 * d), lambda b, i: (b, 0, 0)),
                  _full_spec((1, d)), _full_spec((1, d)),
                  _full_spec(w_in.shape), _full_spec(w_out.shape), _full_spec(pool_w.shape),
                  _full_spec((1, pw)), _full_spec((1, pw)), _full_spec((1, pw))],
        out_specs=row_spec,
        out_shape=jax.ShapeDtypeStruct(res.shape, F32),
        scratch_shapes=[pltpu.VMEM((MAX_WINDOW + tt, pw), F32) for _ in range(4)]
        + [pltpu.VMEM((nh, HGRN_HEAD, HGRN_HEAD), F32)]
        + [pltpu.VMEM((tt, pw), F32) for _ in range(4)],
        compiler_params=_params(),
        name="even_layer",
    )(res, ada.reshape(bsz, 1, 3 * d), pre_g.reshape(1, d), post_g.reshape(1, d),
      w_in.astype(BF16), w_out.astype(BF16), pool_w.astype(BF16),
      pool_scale.reshape(1, pw), lb.reshape(1, pw), onorm_g.reshape(1, pw))


def _head_sum_matrix(d):
    r = _iota((d, d), 0) // RWKV_HEAD
    c = _iota((d, d), 1) // RWKV_HEAD
    return jnp.where(r == c, 1.0, 0.0).astype(BF16)


def _odd_pre_kernel(has_vfirst, *refs):
    if has_vfirst:
        (res_ref, ada_ref, preg_ref, mu_ref, w_ref, w0_ref, w1_ref, w2_ref, a0_ref, a1_ref, a2_ref,
         kk_ref, ka_ref, vf_ref, v0_ref, v1_ref, v2_ref,
         r_out, lw_out, k_out, v_out, kk_out, b_out, z_out, prev_ref) = refs
    else:
        (res_ref, ada_ref, preg_ref, mu_ref, w_ref, w0_ref, w1_ref, w2_ref, a0_ref, a1_ref, a2_ref,
         kk_ref, ka_ref,
         r_out, lw_out, k_out, v_out, kk_out, b_out, z_out, prev_ref) = refs
    ti = pl.program_id(1)
    tt, d = res_ref.shape[1], res_ref.shape[2]

    @pl.when(ti == 0)
    def _():
        prev_ref[...] = jnp.zeros(prev_ref.shape, F32)

    x = res_ref[0]
    ada = ada_ref[0]
    shift, scale = ada[:, :d], ada[:, d:2 * d]
    h = _rms(x, preg_ref[...]) * (1.0 + scale) + shift
    row = _iota((tt, 1), 0)
    hs = jnp.where(row == 0, prev_ref[0:1, :], pltpu.roll(h, 1, axis=0))
    prev_ref[0:1, :] = h[tt - 1:tt, :]
    xx = hs - h
    xm = lambda p: h + xx * mu_ref[p:p + 1, :]

    r = _dot(xm(0), w_ref[0])
    k = _dot(xm(1), w_ref[1])
    xv = xm(2)
    v = _dot(xv, w_ref[2])
    z_out[0] = _dot(xm(3), w_ref[3])
    logw = -_softplus(-(w0_ref[...] + _dot(jnp.tanh(_dot(xm(4), w1_ref[...])), w2_ref[...]))) - 0.5
    lw_out[0] = -jnp.exp(logw)
    a = _sigmoid(a0_ref[...] + _dot(_dot(xm(5), a1_ref[...]), a2_ref[...]))
    if has_vfirst:
        v = v + (vf_ref[0] - v) * _sigmoid(v0_ref[...] + _dot(_dot(xv, v1_ref[...]), v2_ref[...]))
    kkr = k * kk_ref[...]
    ss = _dot_sel_r(kkr * kkr, _head_sum_matrix(d))
    kk = kkr / jnp.maximum(jnp.sqrt(ss), 1e-12)
    r_out[0] = r
    k_out[0] = k * (1.0 + (a - 1.0) * ka_ref[...])
    v_out[0] = v
    kk_out[0] = kk
    b_out[0] = kk * a


def _odd_pre(res, ada, pre_g, mu, w_rkvz, w0, w1, w2, a0, a1, a2, k_k, k_a, vfirst=None):
    bsz, t, d = res.shape
    tt = min(ROW_TILE, t)
    row_spec = pl.BlockSpec((1, tt, d), lambda b, i: (b, i, 0))
    vec = lambda a: a.reshape(1, d)
    args = [res, ada.reshape(bsz, 1, 3 * d), vec(pre_g), mu, w_rkvz.astype(BF16), vec(w0),
            w1.astype(BF16), w2.astype(BF16), vec(a0), a1.astype(BF16), a2.astype(BF16),
            vec(k_k), vec(k_a)]
    specs = [row_spec, pl.BlockSpec((1, 1, 3 * d), lambda b, i: (b, 0, 0)), _full_spec((1, d)),
             _full_spec(mu.shape), _full_spec(w_rkvz.shape), _full_spec((1, d)),
             _full_spec(w1.shape), _full_spec(w2.shape), _full_spec((1, d)),
             _full_spec(a1.shape), _full_spec(a2.shape), _full_spec((1, d)), _full_spec((1, d))]
    if vfirst is not None:
        v_first, v0, v1, v2 = vfirst
        args += [v_first, vec(v0), v1.astype(BF16), v2.astype(BF16)]
        specs += [row_spec, _full_spec((1, d)), _full_spec(v1.shape), _full_spec(v2.shape)]
    return pl.pallas_call(
        functools.partial(_odd_pre_kernel, vfirst is not None),
        grid=(bsz, t // tt),
        in_specs=specs,
        out_specs=[row_spec] * 7,
        out_shape=[jax.ShapeDtypeStruct(res.shape, F32)] * 7,
        scratch_shapes=[pltpu.VMEM((8, d), F32)],
        compiler_params=_params(),
        name="rwkv_pre",
    )(*args)


def _block_rows(y, n_heads, head):
    yb = y.astype(BF16)
    lane_head = _iota(yb.shape, 1) // head
    return jnp.concatenate([jnp.where(lane_head == hh, yb, jnp.zeros_like(yb)) for hh in range(n_heads)],
                           axis=0)


def _rwkv_chunk(r, lw, k, v, kk, b, m_state):
    c_len, g = r.shape
    nh = g // RWKV_HEAD
    hd = RWKV_HEAD
    bd = lambda y: _block_rows(y, nh, hd)

    rr = _iota((c_len, c_len), 0)
    cc = _iota((c_len, c_len), 1)
    tri = jnp.where(cc <= rr, 1.0, 0.0).astype(BF16)
    c = _dot_sel(tri, lw)
    c_last = c[c_len - 1:c_len, :]
    e_in = jnp.exp(c)
    e_out = jnp.exp(-c)
    rt = r * e_in
    kkt = kk * jnp.exp(c - lw)
    kh = k * e_out
    bh = b * e_out
    e_end = jnp.exp(c_last - c)
    kvec = k * e_end
    bvec = b * e_end
    p_end = jnp.exp(c_last)

    t_idx = _iota((c_len, nh * c_len), 0)
    s_idx = _iota((c_len, nh * c_len), 1) % c_len
    strict = s_idx < t_idx
    incl = s_idx <= t_idx
    diag_blk = strict & ((s_idx // RWKV_SUB) == (t_idx // RWKV_SUB))
    eye = jnp.where(s_idx == t_idx, 1.0, 0.0)

    bh_b, kh_b = bd(bh), bd(kh)
    a_kb = _dot_tb(kkt, bh_b)
    a_kk = jnp.where(strict, _dot_tb(kkt, kh_b), 0.0)
    a_rb = jnp.where(incl, _dot_tb(rt, bh_b), 0.0)
    a_rk = jnp.where(incl, _dot_tb(rt, kh_b), 0.0)

    mm = lambda x, y: _dot(x, bd(y))
    n_d = jnp.where(diag_blk, a_kb, 0.0)
    n_o = jnp.where(strict & ~diag_blk, a_kb, 0.0)
    t_d = eye - n_d
    n_p = n_d
    for _ in range(RWKV_SUB.bit_length() - 2):
        n_p = mm(n_p, n_p)
        t_d = t_d + mm(t_d, n_p)
    zz = mm(t_d, n_o)
    t_m = eye - zz
    z_p = zz
    for _ in range((c_len // RWKV_SUB).bit_length() - 2):
        z_p = mm(z_p, z_p)
        t_m = t_m + mm(t_m, z_p)
    tmat = mm(t_m, t_d)

    v_b = bd(v)
    w1 = _dot(tmat, bd(kkt))
    w2 = _dot(tmat, bd(_dot(a_kk, v_b)))
    q_t = rt - _dot(a_rb, bd(w1))
    y_loc = _dot(a_rk, v_b) - _dot(a_rb, bd(w2))
    blk = (_iota((g, g), 0) // hd) == (_iota((g, g), 1) // hd)
    g_m = jnp.where(blk, _dot_ta(bvec, w1), 0.0)
    h_m = jnp.where(blk, _dot_ta(kvec, v) - _dot_ta(bvec, w2), 0.0)
    y = _dot(q_t, m_state) + y_loc
    p_col = jnp.sum(jnp.where(_iota((g, g), 0) == _iota((g, g), 1), p_end, 0.0), axis=1, keepdims=True)
    new_state = p_col * m_state - _dot(g_m, m_state) + h_m
    return y, new_state


def _scan_kernel(r_ref, lw_ref, k_ref, v_ref, kk_ref, b_ref, y_ref, m_ref):
    ti = pl.program_id(1)
    tt, d = r_ref.shape[1], r_ref.shape[2]
    ng = d // GROUP_LANES

    @pl.when(ti == 0)
    def _():
        m_ref[...] = jnp.zeros(m_ref.shape, F32)

    def chunk_body(ci, carry):
        r0 = pl.multiple_of(ci * RWKV_CHUNK, RWKV_CHUNK)
        rows = pl.ds(r0, RWKV_CHUNK)
        for gi in range(ng):
            sl = slice(gi * GROUP_LANES, (gi + 1) * GROUP_LANES)
            y, m_new = _rwkv_chunk(r_ref[0, rows, sl], lw_ref[0, rows, sl], k_ref[0, rows, sl],
                                   v_ref[0, rows, sl], kk_ref[0, rows, sl], b_ref[0, rows, sl],
                                   m_ref[gi])
            y_ref[0, rows, sl] = y
            m_ref[gi] = m_new
        return carry

    lax.fori_loop(0, tt // RWKV_CHUNK, chunk_body, 0)


def _rwkv_scan(r, lw, k, v, kk, b):
    bsz, t, d = r.shape
    tt = min(ROW_TILE, t)
    row_spec = pl.BlockSpec((1, tt, d), lambda bi, i: (bi, i, 0))
    return pl.pallas_call(
        _scan_kernel,
        grid=(bsz, t // tt),
        in_specs=[row_spec] * 6,
        out_specs=row_spec,
        out_shape=jax.ShapeDtypeStruct(r.shape, F32),
        scratch_shapes=[pltpu.VMEM((d // GROUP_LANES, GROUP_LANES, GROUP_LANES), F32)],
        compiler_params=_params(),
        name="rwkv_scan",
    )(r, lw, k, v, kk, b)


def _odd_post_kernel(res_ref, ada_ref, y_ref, r_ref, k_ref, v_ref, z_ref, rk_ref, lg_ref, lb_ref,
                     postg_ref, wout_ref, out_ref):
    d = res_ref.shape[2]
    hs = _head_sum_matrix(d)
    inv = 1.0 / RWKV_HEAD
    y = y_ref[0]
    mean = _dot_sel_r(y, hs) * inv
    yc = y - mean
    var = _dot_sel_r(yc * yc, hs) * inv
    yn = yc * lax.rsqrt(var + LNX_EPS) * lg_ref[...] + lb_ref[...]
    bonus = _dot_sel_r(r_ref[0] * k_ref[0] * rk_ref[...], hs) * v_ref[0]
    out = _dot((yn + bonus) * _silu(z_ref[0]), wout_ref[...])
    gate = ada_ref[0][:, 2 * d:]
    out_ref[0] = res_ref[0] + gate * _rms(out, postg_ref[...])


def _odd_post(res, ada, y, r, k, v, z, r_k, lnx_g, lnx_b, post_g, w_out):
    bsz, t, d = res.shape
    tt = min(ROW_TILE, t)
    row_spec = pl.BlockSpec((1, tt, d), lambda b, i: (b, i, 0))
    vec = lambda a: a.reshape(1, d)
    return pl.pallas_call(
        _odd_post_kernel,
        grid=(bsz, t // tt),
        in_specs=[row_spec, pl.BlockSpec((1, 1, 3 * d), lambda b, i: (b, 0, 0))] + [row_spec] * 5
        + [_full_spec((1, d))] * 4 + [_full_spec(w_out.shape)],
        out_specs=row_spec,
        out_shape=jax.ShapeDtypeStruct(res.shape, F32),
        compiler_params=_params(),
        name="rwkv_post",
    )(res, ada.reshape(bsz, 1, 3 * d), y, r, k, v, z, vec(r_k), vec(lnx_g), vec(lnx_b), vec(post_g),
      w_out.astype(BF16))


def kernel(x, c, ada_w, ada_b, pre_g, post_g, ev_w_in, ev_w_out, pool_w, pool_scale, hgrn_lb_logits,
           hgrn_onorm_g, rw_mu, rw_w_rkvz, rw_w0, rw_w1, rw_w2, rw_a0, rw_a1, rw_a2, rw_k_k, rw_k_a,
           rw_r_k, rw_lnx_g, rw_lnx_b, rw_w_out, rw_v0, rw_v1, rw_v2):
    depth = ada_w.shape[0]
    res = x.astype(F32)
    ada = _ada_all(c, ada_w, ada_b)
    lb_all = _lower_bounds(hgrn_lb_logits)
    v_first = None
    for layer in range(depth):
        j = layer // 2
        if layer % 2 == 0:
            res = _even_layer(res, ada[layer], pre_g[layer], post_g[layer], ev_w_in[j], ev_w_out[j],
                              pool_w[j], pool_scale[j], lb_all[j], hgrn_onorm_g[j])
        else:
            vfirst = None if v_first is None else (v_first, rw_v0[j - 1], rw_v1[j - 1], rw_v2[j - 1])
            r, lw, k, v, kk, b, z = _odd_pre(res, ada[layer], pre_g[layer], rw_mu[j], rw_w_rkvz[j],
                                             rw_w0[j], rw_w1[j], rw_w2[j], rw_a0[j], rw_a1[j],
                                             rw_a2[j], rw_k_k[j], rw_k_a[j], vfirst)
            if v_first is None:
                v_first = v
            y = _rwkv_scan(r, lw, k, v, kk, b)
            res = _odd_post(res, ada[layer], y, r, k, v, z, rw_r_k[j], rw_lnx_g[j], rw_lnx_b[j],
                            post_g[layer], rw_w_out[j])
    return res.astype(x.dtype)
```

```python
import functools

import jax
import jax.numpy as jnp
from jax import lax
from jax.experimental import pallas as pl
from jax.experimental.pallas import tpu as pltpu

F32 = jnp.float32
BF16 = jnp.bfloat16

NORM_EPS = 1e-6
LNX_EPS = 64e-5
POOL_WINDOWS = (2, 4, 8, 16)
MAX_WINDOW = 16
HGRN_HEAD = 128
RWKV_HEAD = 64
RWKV_CHUNK = 64
RWKV_SUB = 16
GROUP_LANES = 256
SCAN_CHUNKS_PER_STEP = 2
ROW_TILE = 256
VMEM_LIMIT_BYTES = 56 * 1024 * 1024


def _dot(a, b):
    return jnp.dot(a.astype(BF16), b.astype(BF16), preferred_element_type=F32)


def _dot_tb(a, b):
    return lax.dot_general(a.astype(BF16), b.astype(BF16), (((1,), (1,)), ((), ())),
                           preferred_element_type=F32)


def _dot_ta(a, b):
    return lax.dot_general(a.astype(BF16), b.astype(BF16), (((0,), (0,)), ((), ())),
                           preferred_element_type=F32)


def _split3(x):
    hi = x.astype(BF16)
    r1 = x - hi.astype(F32)
    mid = r1.astype(BF16)
    lo = (r1 - mid.astype(F32)).astype(BF16)
    return hi, mid, lo


def _dot_sel(sel, x):
    hi, mid, lo = _split3(x)
    f = lambda p: jnp.dot(sel, p, preferred_element_type=F32)
    return (f(lo) + f(mid)) + f(hi)


def _dot_sel_r(x, sel):
    hi = x.astype(BF16)
    lo = (x - hi.astype(F32)).astype(BF16)
    f = lambda p: jnp.dot(p, sel, preferred_element_type=F32)
    return f(lo) + f(hi)


def _sigmoid(x):
    return 1.0 / (1.0 + jnp.exp(-x))


def _silu(x):
    return x * _sigmoid(x)


def _softplus(x):
    return jnp.maximum(x, 0.0) + jnp.log1p(jnp.exp(-jnp.abs(x)))


def _rms(x, g):
    return x * lax.rsqrt(jnp.mean(x * x, axis=-1, keepdims=True) + NORM_EPS) * g


def _iota(shape, dim):
    return lax.broadcasted_iota(jnp.int32, shape, dim)


def _full_spec(shape):
    nd = len(shape)
    return pl.BlockSpec(shape, lambda *_: (0,) * nd)


def _params(**flags):
    return pltpu.CompilerParams(dimension_semantics=("arbitrary", "arbitrary"),
                                vmem_limit_bytes=VMEM_LIMIT_BYTES, flags=flags or None)


def _ada_kernel(c_ref, w_ref, b_ref, o_ref):
    cond = _silu(c_ref[...].astype(F32))
    o_ref[0] = jnp.dot(cond, w_ref[0].astype(F32), preferred_element_type=F32,
                       precision=lax.Precision.HIGHEST) + b_ref[0]


def _ada_all(c, ada_w, ada_b):
    depth, d, d3 = ada_w.shape
    bsz = c.shape[0]
    tn = 1024
    return pl.pallas_call(
        _ada_kernel,
        grid=(depth, d3 // tn),
        in_specs=[pl.BlockSpec((bsz, d), lambda l, n: (0, 0)),
                  pl.BlockSpec((1, d, tn), lambda l, n: (l, 0, n)),
                  pl.BlockSpec((1, 1, tn), lambda l, n: (l, 0, n))],
        out_specs=pl.BlockSpec((1, bsz, tn), lambda l, n: (l, 0, n)),
        out_shape=jax.ShapeDtypeStruct((depth, bsz, d3), F32),
        compiler_params=_params(),
        name="ada_ln",
    )(c, ada_w, ada_b.reshape(depth, 1, d3))


def _lb_kernel(logit_ref, o_ref):
    x = logit_ref[...].astype(F32)
    n = x.shape[0]
    m = jnp.max(x, axis=0, keepdims=True)
    e = jnp.exp(x - m)
    sm = e / jnp.sum(e, axis=0, keepdims=True)
    acc = jnp.zeros_like(sm[0:1])
    for i in range(n):
        if i > 0:
            acc = acc + sm[i:i + 1]
        o_ref[i:i + 1, :] = acc


def _lower_bounds(logits):
    return pl.pallas_call(
        _lb_kernel,
        out_shape=jax.ShapeDtypeStruct(logits.shape, F32),
        name="hgrn_lower_bounds",
    )(logits)


def _even_kernel(res_ref, ada_ref, preg_ref, postg_ref, win_ref, wout_ref, poolw_ref, pscale_ref,
                 lb_ref, og_ref, out_ref,
                 pbuf, kbuf, cbuf, vbuf, st_ref, qt_ref, kv_ref, dl_ref, oin_ref):
    ti = pl.program_id(1)
    tt, d = res_ref.shape[1], res_ref.shape[2]
    pw = pbuf.shape[1]
    nh = pw // HGRN_HEAD
    sub = MAX_WINDOW
    pad = MAX_WINDOW

    @pl.when(ti == 0)
    def _():
        zpad = jnp.zeros((pad, pw), F32)
        pbuf[0:pad, :] = zpad
        kbuf[0:pad, :] = zpad
        cbuf[0:pad, :] = zpad
        vbuf[0:pad, :] = zpad
        st_ref[...] = jnp.zeros(st_ref.shape, F32)

    x = res_ref[0]
    ada = ada_ref[0]
    shift, scale, gate = ada[:, :d], ada[:, d:2 * d], ada[:, 2 * d:]
    h = _rms(x, preg_ref[...]) * (1.0 + scale) + shift
    u = _dot(h, win_ref[...])

    row = _iota((tt, 1), 0)
    up = u[:, :pw]
    pbuf[pad:pad + tt, :] = up
    pos1 = (ti * tt + row + 1).astype(F32)
    gw = pw // len(POOL_WINDOWS)
    pooled = []
    for gi, win in enumerate(POOL_WINDOWS):
        sl = slice(gi * gw, (gi + 1) * gw)
        s = up[:, sl]
        for i in range(1, win):
            s = s + pbuf[pad - i:pad - i + tt, sl]
        p = s / jnp.minimum(pos1, float(win)) - up[:, sl]
        pooled.append(_dot(p, poolw_ref[gi]))
    pbuf[0:pad, :] = pbuf[tt:tt + pad, :]
    y_pool = jnp.concatenate(pooled, axis=-1) * pscale_ref[...]

    q = _silu(u[:, pw:2 * pw])
    fr = u[:, 2 * pw:3 * pw]
    v = u[:, 3 * pw:4 * pw]
    z = u[:, 4 * pw:]
    lb = lb_ref[...]
    k = (1.0 - lb) * _sigmoid(-fr)
    la = jnp.log(lb)
    lsig = jnp.minimum(fr, 0.0) - jnp.log1p(jnp.exp(-jnp.abs(fr)))
    lbb = jnp.log1p(-lb) + lsig
    logf = jnp.maximum(la, lbb) + jnp.log1p(jnp.exp(-jnp.abs(la - lbb)))

    r2 = _iota((tt, tt), 0)
    c2 = _iota((tt, tt), 1)
    same = (r2 // sub) == (c2 // sub)
    tri = jnp.where(same & (c2 <= r2), 1.0, 0.0).astype(BF16)
    blk = jnp.where(same, 1.0, 0.0).astype(BF16)
    cs = _dot_sel(tri, logf)
    csl = _dot_sel(blk, logf)
    qt_ref[...] = q * jnp.exp(cs)
    kv_ref[...] = k * jnp.exp(csl - cs)
    dl_ref[...] = jnp.exp(csl)
    kbuf[pad:pad + tt, :] = k
    cbuf[pad:pad + tt, :] = cs
    vbuf[pad:pad + tt, :] = v

    def sub_body(m, carry):
        r0 = pl.multiple_of(m * sub, sub)
        qtm = qt_ref[pl.ds(r0, sub), :]
        kvm = kv_ref[pl.ds(r0, sub), :]
        vm = vbuf[pl.ds(pad + r0, sub), :]
        dl = dl_ref[pl.ds(r0, 1), :]
        for hh in range(nh):
            sl = slice(hh * HGRN_HEAD, (hh + 1) * HGRN_HEAD)
            st = st_ref[hh]
            oin_ref[pl.ds(r0, sub), sl] = _dot_tb(qtm[:, sl], st)
            st_ref[hh] = st * dl[:, sl] + _dot_ta(vm[:, sl], kvm[:, sl])
        return carry

    lax.fori_loop(0, tt // sub, sub_body, 0)

    tloc = row % sub
    o_heads = [oin_ref[:, hh * HGRN_HEAD:(hh + 1) * HGRN_HEAD] for hh in range(nh)]
    for dlt in range(sub):
        valid = tloc >= dlt
        ks = kbuf[pad - dlt:pad - dlt + tt, :]
        css = cbuf[pad - dlt:pad - dlt + tt, :]
        vs = vbuf[pad - dlt:pad - dlt + tt, :]
        p = q * ks * jnp.exp(jnp.where(valid, cs - css, 0.0))
        for hh in range(nh):
            sl = slice(hh * HGRN_HEAD, (hh + 1) * HGRN_HEAD)
            sc = jnp.where(valid, jnp.sum(p[:, sl], axis=-1, keepdims=True), 0.0)
            o_heads[hh] = o_heads[hh] + sc * vs[:, sl]
    og = og_ref[...]
    y_h = [o * lax.rsqrt(jnp.mean(o * o, axis=-1, keepdims=True) + NORM_EPS)
           * og[:, hh * HGRN_HEAD:(hh + 1) * HGRN_HEAD] for hh, o in enumerate(o_heads)]

    y = jnp.concatenate([y_pool] + y_h, axis=-1) * _silu(z)
    out = _dot(y, wout_ref[...])
    out_ref[0] = x + gate * _rms(out, postg_ref[...])


def _even_layer(res, ada, pre_g, post_g, w_in, w_out, pool_w, pool_scale, lb, onorm_g):
    bsz, t, d = res.shape
    tt = min(ROW_TILE, t)
    pw = pool_scale.shape[-1]
    nh = pw // HGRN_HEAD
    row_spec = pl.BlockSpec((1, tt, d), lambda b, i: (b, i, 0))
    return pl.pallas_call(
        _even_kernel,
        grid=(bsz, t // tt),
        in_specs=[row_spec,
                  pl.BlockSpec((1, 1, 3 * d), lambda b, i: (b, 0, 0)),
                  _full_spec((1, d)), _full_spec((1, d)),
                  _full_spec(w_in.shape), _full_spec(w_out.shape), _full_spec(pool_w.shape),
                  _full_spec((1, pw)), _full_spec((1, pw)), _full_spec((1, pw))],
        out_specs=row_spec,
        out_shape=jax.ShapeDtypeStruct(res.shape, F32),
        scratch_shapes=[pltpu.VMEM((MAX_WINDOW + tt, pw), F32) for _ in range(4)]
        + [pltpu.VMEM((nh, HGRN_HEAD, HGRN_HEAD), F32)]
        + [pltpu.VMEM((tt, pw), F32) for _ in range(4)],
        compiler_params=_params(),
        name="even_layer",
    )(res, ada.reshape(bsz, 1, 3 * d), pre_g.reshape(1, d), post_g.reshape(1, d),
      w_in.astype(BF16), w_out.astype(BF16), pool_w.astype(BF16),
      pool_scale.reshape(1, pw), lb.reshape(1, pw), onorm_g.reshape(1, pw))


def _head_sum_matrix(d):
    r = _iota((d, d), 0) // RWKV_HEAD
    c = _iota((d, d), 1) // RWKV_HEAD
    return jnp.where(r == c, 1.0, 0.0).astype(BF16)


def _odd_pre_kernel(has_vfirst, *refs):
    if has_vfirst:
        (res_ref, ada_ref, preg_ref, mu_ref, w_ref, w0_ref, w1_ref, w2_ref, a0_ref, a1_ref, a2_ref,
         kk_ref, ka_ref, vf_ref, v0_ref, v1_ref, v2_ref,
         r_out, lw_out, k_out, v_out, kk_out, b_out, z_out, prev_ref) = refs
    else:
        (res_ref, ada_ref, preg_ref, mu_ref, w_ref, w0_ref, w1_ref, w2_ref, a0_ref, a1_ref, a2_ref,
         kk_ref, ka_ref,
         r_out, lw_out, k_out, v_out, kk_out, b_out, z_out, prev_ref) = refs
    ti = pl.program_id(1)
    tt, d = res_ref.shape[1], res_ref.shape[2]

    @pl.when(ti == 0)
    def _():
        prev_ref[...] = jnp.zeros(prev_ref.shape, F32)

    x = res_ref[0]
    ada = ada_ref[0]
    shift, scale = ada[:, :d], ada[:, d:2 * d]
    h = _rms(x, preg_ref[...]) * (1.0 + scale) + shift
    row = _iota((tt, 1), 0)
    hs = jnp.where(row == 0, prev_ref[0:1, :], pltpu.roll(h, 1, axis=0))
    prev_ref[0:1, :] = h[tt - 1:tt, :]
    xx = hs - h
    xm = lambda p: h + xx * mu_ref[p:p + 1, :]

    r = _dot(xm(0), w_ref[0])
    k = _dot(xm(1), w_ref[1])
    xv = xm(2)
    v = _dot(xv, w_ref[2])
    z_out[0] = _dot(xm(3), w_ref[3])
    logw = -_softplus(-(w0_ref[...] + _dot(jnp.tanh(_dot(xm(4), w1_ref[...])), w2_ref[...]))) - 0.5
    lw_out[0] = -jnp.exp(logw)
    a = _sigmoid(a0_ref[...] + _dot(_dot(xm(5), a1_ref[...]), a2_ref[...]))
    if has_vfirst:
        v = v + (vf_ref[0] - v) * _sigmoid(v0_ref[...] + _dot(_dot(xv, v1_ref[...]), v2_ref[...]))
    kkr = k * kk_ref[...]
    ss = _dot_sel_r(kkr * kkr, _head_sum_matrix(d))
    kk = kkr / jnp.maximum(jnp.sqrt(ss), 1e-12)
    r_out[0] = r
    k_out[0] = k * (1.0 + (a - 1.0) * ka_ref[...])
    v_out[0] = v
    kk_out[0] = kk
    b_out[0] = kk * a


def _odd_pre(res, ada, pre_g, mu, w_rkvz, w0, w1, w2, a0, a1, a2, k_k, k_a, vfirst=None):
    bsz, t, d = res.shape
    tt = min(ROW_TILE, t)
    row_spec = pl.BlockSpec((1, tt, d), lambda b, i: (b, i, 0))
    vec = lambda a: a.reshape(1, d)
    args = [res, ada.reshape(bsz, 1, 3 * d), vec(pre_g), mu, w_rkvz.astype(BF16), vec(w0),
            w1.astype(BF16), w2.astype(BF16), vec(a0), a1.astype(BF16), a2.astype(BF16),
            vec(k_k), vec(k_a)]
    specs = [row_spec, pl.BlockSpec((1, 1, 3 * d), lambda b, i: (b, 0, 0)), _full_spec((1, d)),
             _full_spec(mu.shape), _full_spec(w_rkvz.shape), _full_spec((1, d)),
             _full_spec(w1.shape), _full_spec(w2.shape), _full_spec((1, d)),
             _full_spec(a1.shape), _full_spec(a2.shape), _full_spec((1, d)), _full_spec((1, d))]
    if vfirst is not None:
        v_first, v0, v1, v2 = vfirst
        args += [v_first, vec(v0), v1.astype(BF16), v2.astype(BF16)]
        specs += [row_spec, _full_spec((1, d)), _full_spec(v1.shape), _full_spec(v2.shape)]
    return pl.pallas_call(
        functools.partial(_odd_pre_kernel, vfirst is not None),
        grid=(bsz, t // tt),
        in_specs=specs,
        out_specs=[row_spec] * 7,
        out_shape=[jax.ShapeDtypeStruct(res.shape, F32)] * 7,
        scratch_shapes=[pltpu.VMEM((8, d), F32)],
        compiler_params=_params(),
        name="rwkv_pre",
    )(*args)


def _block_rows(y, n_heads, head):
    yb = y.astype(BF16)
    lane_head = _iota(yb.shape, 1) // head
    return jnp.concatenate([jnp.where(lane_head == hh, yb, jnp.zeros_like(yb)) for hh in range(n_heads)],
                           axis=0)


def _rwkv_chunk_local(rs, lws, ks, vs, kks, bs):
    c_len, g = rs[0].shape
    nh = g // RWKV_HEAD
    hd = RWKV_HEAD
    each = lambda f, *ls: [f(*xs) for xs in zip(*ls)]
    bd = lambda y: _block_rows(y, nh, hd)

    rr = _iota((c_len, c_len), 0)
    cc = _iota((c_len, c_len), 1)
    tri = jnp.where(cc <= rr, 1.0, 0.0).astype(BF16)
    cs = each(lambda lw: _dot_sel(tri, lw), lws)
    c_last = [c[c_len - 1:c_len, :] for c in cs]
    rts = each(lambda r, c: r * jnp.exp(c), rs, cs)
    kkts = each(lambda kk, c, lw: kk * jnp.exp(c - lw), kks, cs, lws)
    khs = each(lambda k, c: k * jnp.exp(-c), ks, cs)
    bhs = each(lambda b, c: b * jnp.exp(-c), bs, cs)
    kvecs = each(lambda k, c, cl: k * jnp.exp(cl - c), ks, cs, c_last)
    bvecs = each(lambda b, c, cl: b * jnp.exp(cl - c), bs, cs, c_last)

    t_idx = _iota((c_len, nh * c_len), 0)
    s_idx = _iota((c_len, nh * c_len), 1) % c_len
    strict = s_idx < t_idx
    incl = s_idx <= t_idx
    diag_blk = strict & ((s_idx // RWKV_SUB) == (t_idx // RWKV_SUB))
    eye = jnp.where(s_idx == t_idx, 1.0, 0.0)

    stack = lambda x, y: jnp.concatenate([x.astype(BF16), y.astype(BF16)], axis=0)
    v_b = each(bd, vs)
    a_all = each(lambda kkt, rt, bh, kh: _dot_tb(stack(kkt, rt), stack(bd(bh), bd(kh))),
                 kkts, rts, bhs, khs)
    wc = nh * c_len
    a_kb = [a[:c_len, :wc] for a in a_all]
    a_kk = [jnp.where(strict, a[:c_len, wc:], 0.0) for a in a_all]
    a_rb = [jnp.where(incl, a[c_len:, :wc], 0.0) for a in a_all]
    a_rk = [jnp.where(incl, a[c_len:, wc:], 0.0) for a in a_all]

    mm = lambda x, y: _dot(x, bd(y))
    n_p = each(lambda a: jnp.where(diag_blk, a, 0.0), a_kb)
    n_o = each(lambda a: jnp.where(strict & ~diag_blk, a, 0.0), a_kb)
    t_d = each(lambda n: eye - n, n_p)
    n_p = each(mm, n_p, n_p)
    for _ in range(RWKV_SUB.bit_length() - 3):
        both = each(lambda t, n: _dot(stack(t, n), bd(n)), t_d, n_p)
        t_d = each(lambda t, s: t + s[:c_len], t_d, both)
        n_p = [s[c_len:] for s in both]
    t_d = each(lambda t, n: t + mm(t, n), t_d, n_p)
    z_p = each(mm, t_d, n_o)
    t_m = each(lambda z: eye - z, z_p)
    for _ in range((c_len // RWKV_SUB).bit_length() - 2):
        z_p = each(mm, z_p, z_p)
        t_m = each(lambda t, z: t + mm(t, z), t_m, z_p)
    tmat = each(mm, t_m, t_d)

    av_both = each(lambda akk, ark, vb: _dot(stack(akk, ark), vb), a_kk, a_rk, v_b)
    w1 = each(lambda t, x: _dot(t, bd(x)), tmat, kkts)
    w2 = each(lambda t, x: _dot(t, bd(x[:c_len])), tmat, av_both)
    q_t = each(lambda rt, a, w: rt - _dot(a, bd(w)), rts, a_rb, w1)
    y_loc = each(lambda x, ab, w: x[c_len:] - _dot(ab, bd(w)), av_both, a_rb, w2)
    blk = (_iota((g, g), 0) // hd) == (_iota((g, g), 1) // hd)
    dia = _iota((g, g), 0) == _iota((g, g), 1)
    g_m = each(lambda bv, w: jnp.where(blk, _dot_ta(bv, w), 0.0), bvecs, w1)
    h_m = each(lambda kv, v, bv, w: jnp.where(blk, _dot_ta(kv, v) - _dot_ta(bv, w), 0.0),
               kvecs, vs, bvecs, w2)
    p_col = each(lambda cl: jnp.sum(jnp.where(dia, jnp.exp(cl), 0.0), axis=1, keepdims=True), c_last)
    return list(zip(q_t, y_loc, g_m, h_m, p_col))


def _scan_kernel(r_ref, lw_ref, k_ref, v_ref, kk_ref, b_ref, y_ref, m_ref):
    ti = pl.program_id(1)
    tt, d = r_ref.shape[1], r_ref.shape[2]
    ng = d // GROUP_LANES
    in_refs = (r_ref, lw_ref, k_ref, v_ref, kk_ref, b_ref)

    @pl.when(ti == 0)
    def _():
        m_ref[...] = jnp.zeros(m_ref.shape, F32)

    def step_body(si, carry):
        base = si * (SCAN_CHUNKS_PER_STEP * RWKV_CHUNK)
        rows = [pl.ds(pl.multiple_of(base + ci * RWKV_CHUNK, RWKV_CHUNK), RWKV_CHUNK)
                for ci in range(SCAN_CHUNKS_PER_STEP)]
        ins = [[ref[0, rw, gi * GROUP_LANES:(gi + 1) * GROUP_LANES] for rw in rows for gi in range(ng)]
               for ref in in_refs]
        local = _rwkv_chunk_local(*ins)
        states = [m_ref[gi] for gi in range(ng)]
        for ci, rw in enumerate(rows):
            loc = local[ci * ng:(ci + 1) * ng]
            prod = [_dot(jnp.concatenate([q_t.astype(BF16), g_m.astype(BF16)], axis=0), m0)
                    for (q_t, _, g_m, _, _), m0 in zip(loc, states)]
            ys = [p[:RWKV_CHUNK] + y_loc for p, (_, y_loc, _, _, _) in zip(prod, loc)]
            states = [p_col * m0 - p[RWKV_CHUNK:] + h_m
                      for p, (_, _, _, h_m, p_col), m0 in zip(prod, loc, states)]
            y_ref[0, rw, :] = jnp.concatenate(ys, axis=-1)
        for gi in range(ng):
            m_ref[gi] = states[gi]
        return carry

    lax.fori_loop(0, tt // (SCAN_CHUNKS_PER_STEP * RWKV_CHUNK), step_body, 0)


def _rwkv_scan(r, lw, k, v, kk, b):
    bsz, t, d = r.shape
    tt = min(ROW_TILE, t)
    row_spec = pl.BlockSpec((1, tt, d), lambda bi, i: (bi, i, 0))
    return pl.pallas_call(
        _scan_kernel,
        grid=(bsz, t // tt),
        in_specs=[row_spec] * 6,
        out_specs=row_spec,
        out_shape=jax.ShapeDtypeStruct(r.shape, F32),
        scratch_shapes=[pltpu.VMEM((d // GROUP_LANES, GROUP_LANES, GROUP_LANES), F32)],
        compiler_params=_params(),
        name="rwkv_scan",
    )(r, lw, k, v, kk, b)


def _odd_post_kernel(res_ref, ada_ref, y_ref, r_ref, k_ref, v_ref, z_ref, rk_ref, lg_ref, lb_ref,
                     postg_ref, wout_ref, out_ref):
    d = res_ref.shape[2]
    hs = _head_sum_matrix(d)
    inv = 1.0 / RWKV_HEAD
    y = y_ref[0]
    mean = _dot_sel_r(y, hs) * inv
    yc = y - mean
    var = _dot_sel_r(yc * yc, hs) * inv
    yn = yc * lax.rsqrt(var + LNX_EPS) * lg_ref[...] + lb_ref[...]
    bonus = _dot_sel_r(r_ref[0] * k_ref[0] * rk_ref[...], hs) * v_ref[0]
    out = _dot((yn + bonus) * _silu(z_ref[0]), wout_ref[...])
    gate = ada_ref[0][:, 2 * d:]
    out_ref[0] = res_ref[0] + gate * _rms(out, postg_ref[...])


def _odd_post(res, ada, y, r, k, v, z, r_k, lnx_g, lnx_b, post_g, w_out):
    bsz, t, d = res.shape
    tt = min(ROW_TILE, t)
    row_spec = pl.BlockSpec((1, tt, d), lambda b, i: (b, i, 0))
    vec = lambda a: a.reshape(1, d)
    return pl.pallas_call(
        _odd_post_kernel,
        grid=(bsz, t // tt),
        in_specs=[row_spec, pl.BlockSpec((1, 1, 3 * d), lambda b, i: (b, 0, 0))] + [row_spec] * 5
        + [_full_spec((1, d))] * 4 + [_full_spec(w_out.shape)],
        out_specs=row_spec,
        out_shape=jax.ShapeDtypeStruct(res.shape, F32),
        compiler_params=_params(),
        name="rwkv_post",
    )(res, ada.reshape(bsz, 1, 3 * d), y, r, k, v, z, vec(r_k), vec(lnx_g), vec(lnx_b), vec(post_g),
      w_out.astype(BF16))


def kernel(x, c, ada_w, ada_b, pre_g, post_g, ev_w_in, ev_w_out, pool_w, pool_scale, hgrn_lb_logits,
           hgrn_onorm_g, rw_mu, rw_w_rkvz, rw_w0, rw_w1, rw_w2, rw_a0, rw_a1, rw_a2, rw_k_k, rw_k_a,
           rw_r_k, rw_lnx_g, rw_lnx_b, rw_w_out, rw_v0, rw_v1, rw_v2):
    depth = ada_w.shape[0]
    res = x.astype(F32)
    ada = _ada_all(c, ada_w, ada_b)
    lb_all = _lower_bounds(hgrn_lb_logits)
    v_first = None
    for layer in range(depth):
        j = layer // 2
        if layer % 2 == 0:
            res = _even_layer(res, ada[layer], pre_g[layer], post_g[layer], ev_w_in[j], ev_w_out[j],
                              pool_w[j], pool_scale[j], lb_all[j], hgrn_onorm_g[j])
        else:
            vfirst = None if v_first is None else (v_first, rw_v0[j - 1], rw_v1[j - 1], rw_v2[j - 1])
            r, lw, k, v, kk, b, z = _odd_pre(res, ada[layer], pre_g[layer], rw_mu[j], rw_w_rkvz[j],
                                             rw_w0[j], rw_w1[j], rw_w2[j], rw_a0[j], rw_a1[j],
                                             rw_a2[j], rw_k_k[j], rw_k_a[j], vfirst)
            if v_first is None:
                v_first = v
            y = _rwkv_scan(r, lw, k, v, kk, b)
            res = _odd_post(res, ada[layer], y, r, k, v, z, rw_r_k[j], rw_lnx_g[j], rw_lnx_b[j],
                            post_g[layer], rw_w_out[j])
    return res.astype(x.dtype)
```

```python
import functools

import jax
import jax.numpy as jnp
from jax import lax
from jax.experimental import pallas as pl
from jax.experimental.pallas import tpu as pltpu

F32 = jnp.float32
BF16 = jnp.bfloat16

NORM_EPS = 1e-6
LNX_EPS = 64e-5
POOL_WINDOWS = (2, 4, 8, 16)
MAX_WINDOW = 16
HGRN_HEAD = 128
RWKV_HEAD = 64
RWKV_CHUNK = 64
RWKV_SUB = 16
HEAD_SLOTS = 128
GROUP_LANES = 256
SCAN_CHUNKS_PER_STEP = 2
ROW_TILE = 256
VMEM_LIMIT_BYTES = 56 * 1024 * 1024


def _dot(a, b):
    return jnp.dot(a.astype(BF16), b.astype(BF16), preferred_element_type=F32)


def _dot_tb(a, b):
    return lax.dot_general(a.astype(BF16), b.astype(BF16), (((1,), (1,)), ((), ())),
                           preferred_element_type=F32)


def _dot_ta(a, b):
    return lax.dot_general(a.astype(BF16), b.astype(BF16), (((0,), (0,)), ((), ())),
                           preferred_element_type=F32)


def _split3(x):
    hi = x.astype(BF16)
    r1 = x - hi.astype(F32)
    mid = r1.astype(BF16)
    lo = (r1 - mid.astype(F32)).astype(BF16)
    return hi, mid, lo


def _dot_sel(sel, x):
    hi, mid, lo = _split3(x)
    f = lambda p: jnp.dot(sel, p, preferred_element_type=F32)
    return (f(lo) + f(mid)) + f(hi)


def _dot_sel_r(x, sel):
    hi = x.astype(BF16)
    lo = (x - hi.astype(F32)).astype(BF16)
    f = lambda p: jnp.dot(p, sel, preferred_element_type=F32)
    return f(lo) + f(hi)


def _sigmoid(x):
    return 1.0 / (1.0 + jnp.exp(-x))


def _silu(x):
    return x * _sigmoid(x)


def _softplus(x):
    return jnp.maximum(x, 0.0) + jnp.log1p(jnp.exp(-jnp.abs(x)))


def _rms(x, g):
    return x * lax.rsqrt(jnp.mean(x * x, axis=-1, keepdims=True) + NORM_EPS) * g


def _iota(shape, dim):
    return lax.broadcasted_iota(jnp.int32, shape, dim)


def _full_spec(shape):
    nd = len(shape)
    return pl.BlockSpec(shape, lambda *_: (0,) * nd)


def _params(**flags):
    return pltpu.CompilerParams(dimension_semantics=("arbitrary", "arbitrary"),
                                vmem_limit_bytes=VMEM_LIMIT_BYTES, flags=flags or None)


def _ada_kernel(c_ref, w_ref, b_ref, o_ref):
    cond = _silu(c_ref[...].astype(F32))
    o_ref[0] = jnp.dot(cond, w_ref[0].astype(F32), preferred_element_type=F32,
                       precision=lax.Precision.HIGHEST) + b_ref[0]


def _ada_all(c, ada_w, ada_b):
    depth, d, d3 = ada_w.shape
    bsz = c.shape[0]
    tn = 1024
    return pl.pallas_call(
        _ada_kernel,
        grid=(depth, d3 // tn),
        in_specs=[pl.BlockSpec((bsz, d), lambda l, n: (0, 0)),
                  pl.BlockSpec((1, d, tn), lambda l, n: (l, 0, n)),
                  pl.BlockSpec((1, 1, tn), lambda l, n: (l, 0, n))],
        out_specs=pl.BlockSpec((1, bsz, tn), lambda l, n: (l, 0, n)),
        out_shape=jax.ShapeDtypeStruct((depth, bsz, d3), F32),
        compiler_params=_params(),
        name="ada_ln",
    )(c, ada_w, ada_b.reshape(depth, 1, d3))


def _lb_kernel(logit_ref, o_ref):
    x = logit_ref[...].astype(F32)
    n = x.shape[0]
    m = jnp.max(x, axis=0, keepdims=True)
    e = jnp.exp(x - m)
    sm = e / jnp.sum(e, axis=0, keepdims=True)
    acc = jnp.zeros_like(sm[0:1])
    for i in range(n):
        if i > 0:
            acc = acc + sm[i:i + 1]
        o_ref[i:i + 1, :] = acc


def _lower_bounds(logits):
    return pl.pallas_call(
        _lb_kernel,
        out_shape=jax.ShapeDtypeStruct(logits.shape, F32),
        name="hgrn_lower_bounds",
    )(logits)


def _even_kernel(res_ref, ada_ref, preg_ref, postg_ref, win_ref, wout_ref, poolw_ref, pscale_ref,
                 lb_ref, og_ref, out_ref,
                 pbuf, st_ref, hq_ref, hk_ref, hc_ref, hv_ref, ho_ref):
    ti = pl.program_id(1)
    tt, d = res_ref.shape[1], res_ref.shape[2]
    pw = pbuf.shape[1]
    nh = pw // HGRN_HEAD
    sub = MAX_WINDOW
    pad = MAX_WINDOW

    @pl.when(ti == 0)
    def _():
        pbuf[0:pad, :] = jnp.zeros((pad, pw), F32)
        st_ref[...] = jnp.zeros(st_ref.shape, F32)

    x = res_ref[0]
    ada = ada_ref[0]
    shift, scale, gate = ada[:, :d], ada[:, d:2 * d], ada[:, 2 * d:]
    h = _rms(x, preg_ref[...]) * (1.0 + scale) + shift
    u = _dot(h, win_ref[...])

    row = _iota((tt, 1), 0)
    up = u[:, :pw]
    pbuf[pad:pad + tt, :] = up
    pos1 = (ti * tt + row + 1).astype(F32)
    gw = pw // len(POOL_WINDOWS)
    pooled = []
    for gi, win in enumerate(POOL_WINDOWS):
        sl = slice(gi * gw, (gi + 1) * gw)
        s = up[:, sl]
        for i in range(1, win):
            s = s + pbuf[pad - i:pad - i + tt, sl]
        p = s / jnp.minimum(pos1, float(win)) - up[:, sl]
        pooled.append(_dot(p, poolw_ref[gi]))
    pbuf[0:pad, :] = pbuf[tt:tt + pad, :]
    y_pool = jnp.concatenate(pooled, axis=-1) * pscale_ref[...]

    q = _silu(u[:, pw:2 * pw])
    fr = u[:, 2 * pw:3 * pw]
    v = u[:, 3 * pw:4 * pw]
    z = u[:, 4 * pw:]
    lb = lb_ref[...]
    k = (1.0 - lb) * _sigmoid(-fr)
    la = jnp.log(lb)
    lsig = jnp.minimum(fr, 0.0) - jnp.log1p(jnp.exp(-jnp.abs(fr)))
    lbb = jnp.log1p(-lb) + lsig
    logf = jnp.maximum(la, lbb) + jnp.log1p(jnp.exp(-jnp.abs(la - lbb)))

    r2 = _iota((tt, tt), 0)
    c2 = _iota((tt, tt), 1)
    same = (r2 // sub) == (c2 // sub)
    tri = jnp.where(same & (c2 <= r2), 1.0, 0.0).astype(BF16)
    blk = jnp.where(same, 1.0, 0.0).astype(BF16)
    cs = _dot_sel(tri, logf)
    csl = _dot_sel(blk, logf)
    qt = q * jnp.exp(cs)
    kvec = k * jnp.exp(csl - cs)
    dl = jnp.exp(csl)
    nsub = tt // sub
    heads = [slice(hh * HGRN_HEAD, (hh + 1) * HGRN_HEAD) for hh in range(nh)]
    subs = [slice(m * sub, (m + 1) * sub) for m in range(nsub)]

    upd = [[_dot_ta(v[rows, hs], kvec[rows, hs]) for hs in heads] for rows in subs]
    start = []
    states = [st_ref[hh] for hh in range(nh)]
    for m, rows in enumerate(subs):
        start.append([s.astype(BF16) for s in states])
        states = [s * dl[m * sub:m * sub + 1, hs] + upd[m][hh]
                  for hh, (s, hs) in enumerate(zip(states, heads))]
    for hh in range(nh):
        st_ref[hh] = states[hh]
    o_inter = [jnp.concatenate([_dot_tb(qt[rows, hs], start[m][hh]) for m, rows in enumerate(subs)],
                               axis=0) for hh, hs in enumerate(heads)]

    for hh, hs in enumerate(heads):
        hq_ref[hh] = q[:, hs]
        hk_ref[hh] = k[:, hs]
        hc_ref[hh] = cs[:, hs]
        hv_ref[hh] = v[:, hs]
    o_heads = []
    for hh in range(nh):
        at = lambda ref, t: ref[hh, pl.ds(t, nsub, stride=sub), :]
        qs = [at(hq_ref, t) for t in range(sub)]
        ks = [at(hk_ref, t) for t in range(sub)]
        cc = [at(hc_ref, t) for t in range(sub)]
        vv = [at(hv_ref, t) for t in range(sub)]
        for t in range(sub):
            acc = jnp.sum(qs[t] * ks[t], axis=-1, keepdims=True) * vv[t]
            for s in range(t):
                p = qs[t] * ks[s] * jnp.exp(cc[t] - cc[s])
                acc = acc + jnp.sum(p, axis=-1, keepdims=True) * vv[s]
            ho_ref[hh, pl.ds(t, nsub, stride=sub), :] = acc
        o_heads.append(o_inter[hh] + ho_ref[hh])
    og = og_ref[...]
    y_h = [o * lax.rsqrt(jnp.mean(o * o, axis=-1, keepdims=True) + NORM_EPS)
           * og[:, hh * HGRN_HEAD:(hh + 1) * HGRN_HEAD] for hh, o in enumerate(o_heads)]

    y = jnp.concatenate([y_pool] + y_h, axis=-1) * _silu(z)
    out = _dot(y, wout_ref[...])
    out_ref[0] = x + gate * _rms(out, postg_ref[...])


def _even_layer(res, ada, pre_g, post_g, w_in, w_out, pool_w, pool_scale, lb, onorm_g):
    bsz, t, d = res.shape
    tt = min(ROW_TILE, t)
    pw = pool_scale.shape[-1]
    nh = pw // HGRN_HEAD
    row_spec = pl.BlockSpec((1, tt, d), lambda b, i: (b, i, 0))
    return pl.pallas_call(
        _even_kernel,
        grid=(bsz, t // tt),
        in_specs=[row_spec,
                  pl.BlockSpec((1, 1, 3 * d), lambda b, i: (b, 0, 0)),
                  _full_spec((1, d)), _full_spec((1, d)),
                  _full_spec(w_in.shape), _full_spec(w_out.shape), _full_spec(pool_w.shape),
                  _full_spec((1, pw)), _full_spec((1, pw)), _full_spec((1, pw))],
        out_specs=row_spec,
        out_shape=jax.ShapeDtypeStruct(res.shape, F32),
        scratch_shapes=[pltpu.VMEM((MAX_WINDOW + tt, pw), F32),
                        pltpu.VMEM((nh, HGRN_HEAD, HGRN_HEAD), F32)]
        + [pltpu.VMEM((nh, tt, HGRN_HEAD), F32) for _ in range(5)],
        compiler_params=_params(),
        name="even_layer",
    )(res, ada.reshape(bsz, 1, 3 * d), pre_g.reshape(1, d), post_g.reshape(1, d),
      w_in.astype(BF16), w_out.astype(BF16), pool_w.astype(BF16),
      pool_scale.reshape(1, pw), lb.reshape(1, pw), onorm_g.reshape(1, pw))


def _head_reduce_matrix(d):
    return jnp.where(_iota((d, HEAD_SLOTS), 0) // RWKV_HEAD == _iota((d, HEAD_SLOTS), 1), 1.0, 0.0).astype(BF16)


def _head_spread_matrix(d):
    return jnp.where(_iota((HEAD_SLOTS, d), 0) == _iota((HEAD_SLOTS, d), 1) // RWKV_HEAD, 1.0, 0.0).astype(BF16)


def _odd_pre_kernel(has_vfirst, *refs):
    if has_vfirst:
        (res_ref, ada_ref, preg_ref, mu_ref, w_ref, w0_ref, w1_ref, w2_ref, a0_ref, a1_ref, a2_ref,
         kk_ref, ka_ref, vf_ref, v0_ref, v1_ref, v2_ref,
         r_out, lw_out, k_out, v_out, kk_out, b_out, z_out, prev_ref) = refs
    else:
        (res_ref, ada_ref, preg_ref, mu_ref, w_ref, w0_ref, w1_ref, w2_ref, a0_ref, a1_ref, a2_ref,
         kk_ref, ka_ref,
         r_out, lw_out, k_out, v_out, kk_out, b_out, z_out, prev_ref) = refs
    ti = pl.program_id(1)
    tt, d = res_ref.shape[1], res_ref.shape[2]

    @pl.when(ti == 0)
    def _():
        prev_ref[...] = jnp.zeros(prev_ref.shape, F32)

    x = res_ref[0]
    ada = ada_ref[0]
    shift, scale = ada[:, :d], ada[:, d:2 * d]
    h = _rms(x, preg_ref[...]) * (1.0 + scale) + shift
    row = _iota((tt, 1), 0)
    hs = jnp.where(row == 0, prev_ref[0:1, :], pltpu.roll(h, 1, axis=0))
    prev_ref[0:1, :] = h[tt - 1:tt, :]
    xx = hs - h
    xm = lambda p: h + xx * mu_ref[p:p + 1, :]

    r = _dot(xm(0), w_ref[0])
    k = _dot(xm(1), w_ref[1])
    xv = xm(2)
    v = _dot(xv, w_ref[2])
    z_out[0] = _dot(xm(3), w_ref[3])
    logw = -_softplus(-(w0_ref[...] + _dot(jnp.tanh(_dot(xm(4), w1_ref[...])), w2_ref[...]))) - 0.5
    lw_out[0] = -jnp.exp(logw)
    a = _sigmoid(a0_ref[...] + _dot(_dot(xm(5), a1_ref[...]), a2_ref[...]))
    if has_vfirst:
        v = v + (vf_ref[0] - v) * _sigmoid(v0_ref[...] + _dot(_dot(xv, v1_ref[...]), v2_ref[...]))
    kkr = k * kk_ref[...]
    ss = _dot(kkr * kkr, _head_reduce_matrix(d))
    kk = kkr * _dot_sel_r(1.0 / jnp.maximum(jnp.sqrt(ss), 1e-12), _head_spread_matrix(d))
    r_out[0] = r
    k_out[0] = k * (1.0 + (a - 1.0) * ka_ref[...])
    v_out[0] = v
    kk_out[0] = kk
    b_out[0] = kk * a


def _odd_pre(res, ada, pre_g, mu, w_rkvz, w0, w1, w2, a0, a1, a2, k_k, k_a, vfirst=None):
    bsz, t, d = res.shape
    tt = min(ROW_TILE, t)
    row_spec = pl.BlockSpec((1, tt, d), lambda b, i: (b, i, 0))
    vec = lambda a: a.reshape(1, d)
    args = [res, ada.reshape(bsz, 1, 3 * d), vec(pre_g), mu, w_rkvz.astype(BF16), vec(w0),
            w1.astype(BF16), w2.astype(BF16), vec(a0), a1.astype(BF16), a2.astype(BF16),
            vec(k_k), vec(k_a)]
    specs = [row_spec, pl.BlockSpec((1, 1, 3 * d), lambda b, i: (b, 0, 0)), _full_spec((1, d)),
             _full_spec(mu.shape), _full_spec(w_rkvz.shape), _full_spec((1, d)),
             _full_spec(w1.shape), _full_spec(w2.shape), _full_spec((1, d)),
             _full_spec(a1.shape), _full_spec(a2.shape), _full_spec((1, d)), _full_spec((1, d))]
    if vfirst is not None:
        v_first, v0, v1, v2 = vfirst
        args += [v_first, vec(v0), v1.astype(BF16), v2.astype(BF16)]
        specs += [row_spec, _full_spec((1, d)), _full_spec(v1.shape), _full_spec(v2.shape)]
    return pl.pallas_call(
        functools.partial(_odd_pre_kernel, vfirst is not None),
        grid=(bsz, t // tt),
        in_specs=specs,
        out_specs=[row_spec] * 7,
        out_shape=[jax.ShapeDtypeStruct(res.shape, F32)] * 7,
        scratch_shapes=[pltpu.VMEM((8, d), F32)],
        compiler_params=_params(),
        name="rwkv_pre",
    )(*args)


def _block_rows(y, n_heads, head):
    yb = y.astype(BF16)
    lane_head = _iota(yb.shape, 1) // head
    return jnp.concatenate([jnp.where(lane_head == hh, yb, jnp.zeros_like(yb)) for hh in range(n_heads)],
                           axis=0)


def _rwkv_chunk_local(rs, lws, ks, vs, kks, bs):
    c_len, g = rs[0].shape
    nh = g // RWKV_HEAD
    hd = RWKV_HEAD
    each = lambda f, *ls: [f(*xs) for xs in zip(*ls)]
    bd = lambda y: _block_rows(y, nh, hd)

    rr = _iota((c_len, c_len), 0)
    cc = _iota((c_len, c_len), 1)
    tri = jnp.where(cc <= rr, 1.0, 0.0).astype(BF16)
    cs = each(lambda lw: _dot_sel(tri, lw), lws)
    c_last = [c[c_len - 1:c_len, :] for c in cs]
    rts = each(lambda r, c: r * jnp.exp(c), rs, cs)
    kkts = each(lambda kk, c, lw: kk * jnp.exp(c - lw), kks, cs, lws)
    khs = each(lambda k, c: k * jnp.exp(-c), ks, cs)
    bhs = each(lambda b, c: b * jnp.exp(-c), bs, cs)
    kvecs = each(lambda k, c, cl: k * jnp.exp(cl - c), ks, cs, c_last)
    bvecs = each(lambda b, c, cl: b * jnp.exp(cl - c), bs, cs, c_last)

    t_idx = _iota((c_len, nh * c_len), 0)
    s_idx = _iota((c_len, nh * c_len), 1) % c_len
    strict = s_idx < t_idx
    incl = s_idx <= t_idx
    diag_blk = strict & ((s_idx // RWKV_SUB) == (t_idx // RWKV_SUB))
    eye = jnp.where(s_idx == t_idx, 1.0, 0.0)

    stack = lambda x, y: jnp.concatenate([x.astype(BF16), y.astype(BF16)], axis=0)
    v_b = each(bd, vs)
    a_all = each(lambda kkt, rt, bh, kh: _dot_tb(stack(kkt, rt), stack(bd(bh), bd(kh))),
                 kkts, rts, bhs, khs)
    wc = nh * c_len
    a_kb = [a[:c_len, :wc] for a in a_all]
    a_kk = [jnp.where(strict, a[:c_len, wc:], 0.0) for a in a_all]
    a_rb = [jnp.where(incl, a[c_len:, :wc], 0.0) for a in a_all]
    a_rk = [jnp.where(incl, a[c_len:, wc:], 0.0) for a in a_all]

    mm = lambda x, y: _dot(x, bd(y))
    n_p = each(lambda a: jnp.where(diag_blk, a, 0.0), a_kb)
    n_o = each(lambda a: jnp.where(strict & ~diag_blk, a, 0.0), a_kb)
    t_d = each(lambda n: eye - n, n_p)
    n_p = each(mm, n_p, n_p)
    for _ in range(RWKV_SUB.bit_length() - 3):
        both = each(lambda t, n: _dot(stack(t, n), bd(n)), t_d, n_p)
        t_d = each(lambda t, s: t + s[:c_len], t_d, both)
        n_p = [s[c_len:] for s in both]
    t_d = each(lambda t, n: t + mm(t, n), t_d, n_p)
    z_p = each(mm, t_d, n_o)
    t_m = each(lambda z: eye - z, z_p)
    for _ in range((c_len // RWKV_SUB).bit_length() - 2):
        z_p = each(mm, z_p, z_p)
        t_m = each(lambda t, z: t + mm(t, z), t_m, z_p)
    tmat = each(mm, t_m, t_d)

    av_both = each(lambda akk, ark, vb: _dot(stack(akk, ark), vb), a_kk, a_rk, v_b)
    w1 = each(lambda t, x: _dot(t, bd(x)), tmat, kkts)
    w2 = each(lambda t, x: _dot(t, bd(x[:c_len])), tmat, av_both)
    q_t = each(lambda rt, a, w: rt - _dot(a, bd(w)), rts, a_rb, w1)
    y_loc = each(lambda x, ab, w: x[c_len:] - _dot(ab, bd(w)), av_both, a_rb, w2)
    blk = (_iota((g, g), 0) // hd) == (_iota((g, g), 1) // hd)
    dia = _iota((g, g), 0) == _iota((g, g), 1)
    g_m = each(lambda bv, w: jnp.where(blk, _dot_ta(bv, w), 0.0), bvecs, w1)
    h_m = each(lambda kv, v, bv, w: jnp.where(blk, _dot_ta(kv, v) - _dot_ta(bv, w), 0.0),
               kvecs, vs, bvecs, w2)
    p_col = each(lambda cl: jnp.sum(jnp.where(dia, jnp.exp(cl), 0.0), axis=1, keepdims=True), c_last)
    return list(zip(q_t, y_loc, g_m, h_m, p_col))


def _scan_kernel(r_ref, lw_ref, k_ref, v_ref, kk_ref, b_ref, y_ref, m_ref):
    ti = pl.program_id(1)
    tt, d = r_ref.shape[1], r_ref.shape[2]
    ng = d // GROUP_LANES
    in_refs = (r_ref, lw_ref, k_ref, v_ref, kk_ref, b_ref)

    @pl.when(ti == 0)
    def _():
        m_ref[...] = jnp.zeros(m_ref.shape, F32)

    def step_body(si, carry):
        base = si * (SCAN_CHUNKS_PER_STEP * RWKV_CHUNK)
        rows = [pl.ds(pl.multiple_of(base + ci * RWKV_CHUNK, RWKV_CHUNK), RWKV_CHUNK)
                for ci in range(SCAN_CHUNKS_PER_STEP)]
        ins = [[ref[0, rw, gi * GROUP_LANES:(gi + 1) * GROUP_LANES] for rw in rows for gi in range(ng)]
               for ref in in_refs]
        local = _rwkv_chunk_local(*ins)
        states = [m_ref[gi] for gi in range(ng)]
        for ci, rw in enumerate(rows):
            loc = local[ci * ng:(ci + 1) * ng]
            prod = [_dot(jnp.concatenate([q_t.astype(BF16), g_m.astype(BF16)], axis=0), m0)
                    for (q_t, _, g_m, _, _), m0 in zip(loc, states)]
            ys = [p[:RWKV_CHUNK] + y_loc for p, (_, y_loc, _, _, _) in zip(prod, loc)]
            states = [p_col * m0 - p[RWKV_CHUNK:] + h_m
                      for p, (_, _, _, h_m, p_col), m0 in zip(prod, loc, states)]
            y_ref[0, rw, :] = jnp.concatenate(ys, axis=-1)
        for gi in range(ng):
            m_ref[gi] = states[gi]
        return carry

    lax.fori_loop(0, tt // (SCAN_CHUNKS_PER_STEP * RWKV_CHUNK), step_body, 0)


def _rwkv_scan(r, lw, k, v, kk, b):
    bsz, t, d = r.shape
    tt = min(ROW_TILE, t)
    row_spec = pl.BlockSpec((1, tt, d), lambda bi, i: (bi, i, 0))
    return pl.pallas_call(
        _scan_kernel,
        grid=(bsz, t // tt),
        in_specs=[row_spec] * 6,
        out_specs=row_spec,
        out_shape=jax.ShapeDtypeStruct(r.shape, F32),
        scratch_shapes=[pltpu.VMEM((d // GROUP_LANES, GROUP_LANES, GROUP_LANES), F32)],
        compiler_params=_params(),
        name="rwkv_scan",
    )(r, lw, k, v, kk, b)


def _odd_post_kernel(res_ref, ada_ref, y_ref, r_ref, k_ref, v_ref, z_ref, rk_ref, lg_ref, lb_ref,
                     postg_ref, wout_ref, out_ref):
    d = res_ref.shape[2]
    tt = res_ref.shape[1]
    red, spread = _head_reduce_matrix(d), _head_spread_matrix(d)
    inv = 1.0 / RWKV_HEAD
    y = y_ref[0]
    rkr = r_ref[0] * k_ref[0] * rk_ref[...]
    sums = _dot(jnp.concatenate([y, rkr], axis=0), red)
    both = _dot_sel_r(jnp.concatenate([sums[:tt] * inv, sums[tt:]], axis=0), spread)
    yc = y - both[:tt]
    rstd = lax.rsqrt(_dot(yc * yc, red) * inv + LNX_EPS)
    yn = yc * _dot_sel_r(rstd, spread) * lg_ref[...] + lb_ref[...]
    out = _dot((yn + both[tt:] * v_ref[0]) * _silu(z_ref[0]), wout_ref[...])
    gate = ada_ref[0][:, 2 * d:]
    out_ref[0] = res_ref[0] + gate * _rms(out, postg_ref[...])


def _odd_post(res, ada, y, r, k, v, z, r_k, lnx_g, lnx_b, post_g, w_out):
    bsz, t, d = res.shape
    tt = min(ROW_TILE, t)
    row_spec = pl.BlockSpec((1, tt, d), lambda b, i: (b, i, 0))
    vec = lambda a: a.reshape(1, d)
    return pl.pallas_call(
        _odd_post_kernel,
        grid=(bsz, t // tt),
        in_specs=[row_spec, pl.BlockSpec((1, 1, 3 * d), lambda b, i: (b, 0, 0))] + [row_spec] * 5
        + [_full_spec((1, d))] * 4 + [_full_spec(w_out.shape)],
        out_specs=row_spec,
        out_shape=jax.ShapeDtypeStruct(res.shape, F32),
        compiler_params=_params(),
        name="rwkv_post",
    )(res, ada.reshape(bsz, 1, 3 * d), y, r, k, v, z, vec(r_k), vec(lnx_g), vec(lnx_b), vec(post_g),
      w_out.astype(BF16))


def kernel(x, c, ada_w, ada_b, pre_g, post_g, ev_w_in, ev_w_out, pool_w, pool_scale, hgrn_lb_logits,
           hgrn_onorm_g, rw_mu, rw_w_rkvz, rw_w0, rw_w1, rw_w2, rw_a0, rw_a1, rw_a2, rw_k_k, rw_k_a,
           rw_r_k, rw_lnx_g, rw_lnx_b, rw_w_out, rw_v0, rw_v1, rw_v2):
    depth = ada_w.shape[0]
    res = x.astype(F32)
    ada = _ada_all(c, ada_w, ada_b)
    lb_all = _lower_bounds(hgrn_lb_logits)
    v_first = None
    for layer in range(depth):
        j = layer // 2
        if layer % 2 == 0:
            res = _even_layer(res, ada[layer], pre_g[layer], post_g[layer], ev_w_in[j], ev_w_out[j],
                              pool_w[j], pool_scale[j], lb_all[j], hgrn_onorm_g[j])
        else:
            vfirst = None if v_first is None else (v_first, rw_v0[j - 1], rw_v1[j - 1], rw_v2[j - 1])
            r, lw, k, v, kk, b, z = _odd_pre(res, ada[layer], pre_g[layer], rw_mu[j], rw_w_rkvz[j],
                                             rw_w0[j], rw_w1[j], rw_w2[j], rw_a0[j], rw_a1[j],
                                             rw_a2[j], rw_k_k[j], rw_k_a[j], vfirst)
            if v_first is None:
                v_first = v
            y = _rwkv_scan(r, lw, k, v, kk, b)
            res = _odd_post(res, ada[layer], y, r, k, v, z, rw_r_k[j], rw_lnx_g[j], rw_lnx_b[j],
                            post_g[layer], rw_w_out[j])
    return res.astype(x.dtype)
```

```python
import functools

import jax
import jax.numpy as jnp
from jax import lax
from jax.experimental import pallas as pl
from jax.experimental.pallas import tpu as pltpu

F32 = jnp.float32
BF16 = jnp.bfloat16

NORM_EPS = 1e-6
LNX_EPS = 64e-5
POOL_WINDOWS = (2, 4, 8, 16)
MAX_WINDOW = 16
HGRN_HEAD = 128
RWKV_HEAD = 64
RWKV_CHUNK = 64
RWKV_SUB = 16
HEAD_SLOTS = 128
GROUP_LANES = 256
SCAN_CHUNKS_PER_STEP = 2
SCAN_BLOCK_STAGGER = 5
ROW_TILE = 256
VMEM_LIMIT_BYTES = 56 * 1024 * 1024


def _dot(a, b):
    return jnp.dot(a.astype(BF16), b.astype(BF16), preferred_element_type=F32)


def _dot_tb(a, b):
    return lax.dot_general(a.astype(BF16), b.astype(BF16), (((1,), (1,)), ((), ())),
                           preferred_element_type=F32)


def _dot_ta(a, b):
    return lax.dot_general(a.astype(BF16), b.astype(BF16), (((0,), (0,)), ((), ())),
                           preferred_element_type=F32)


def _split3(x):
    hi = x.astype(BF16)
    r1 = x - hi.astype(F32)
    mid = r1.astype(BF16)
    lo = (r1 - mid.astype(F32)).astype(BF16)
    return hi, mid, lo


def _dot_sel(sel, x):
    hi, mid, lo = _split3(x)
    f = lambda p: jnp.dot(sel, p, preferred_element_type=F32)
    return (f(lo) + f(mid)) + f(hi)


def _dot_sel_r(x, sel):
    hi = x.astype(BF16)
    lo = (x - hi.astype(F32)).astype(BF16)
    f = lambda p: jnp.dot(p, sel, preferred_element_type=F32)
    return f(lo) + f(hi)


def _sigmoid(x):
    return jax.nn.sigmoid(x)


def _silu(x):
    return x * _sigmoid(x)


def _softplus(x):
    return jnp.maximum(x, 0.0) + jnp.log1p(jnp.exp(-jnp.abs(x)))


def _rms(x, g):
    return x * lax.rsqrt(jnp.mean(x * x, axis=-1, keepdims=True) + NORM_EPS) * g


def _iota(shape, dim):
    return lax.broadcasted_iota(jnp.int32, shape, dim)


def _full_spec(shape):
    nd = len(shape)
    return pl.BlockSpec(shape, lambda *_: (0,) * nd)


def _params(**flags):
    return pltpu.CompilerParams(dimension_semantics=("arbitrary", "arbitrary"),
                                vmem_limit_bytes=VMEM_LIMIT_BYTES, flags=flags or None)


def _ada_kernel(c_ref, w_ref, b_ref, o_ref):
    cond = _silu(c_ref[...].astype(F32))
    o_ref[0] = jnp.dot(cond, w_ref[0].astype(F32), preferred_element_type=F32,
                       precision=lax.Precision.HIGHEST) + b_ref[0]


def _ada_all(c, ada_w, ada_b):
    depth, d, d3 = ada_w.shape
    bsz = c.shape[0]
    tn = 1024
    return pl.pallas_call(
        _ada_kernel,
        grid=(depth, d3 // tn),
        in_specs=[pl.BlockSpec((bsz, d), lambda l, n: (0, 0)),
                  pl.BlockSpec((1, d, tn), lambda l, n: (l, 0, n)),
                  pl.BlockSpec((1, 1, tn), lambda l, n: (l, 0, n))],
        out_specs=pl.BlockSpec((1, bsz, tn), lambda l, n: (l, 0, n)),
        out_shape=jax.ShapeDtypeStruct((depth, bsz, d3), F32),
        compiler_params=_params(),
        name="ada_ln",
    )(c, ada_w, ada_b.reshape(depth, 1, d3))


def _lb_kernel(logit_ref, o_ref):
    x = logit_ref[...].astype(F32)
    n = x.shape[0]
    m = jnp.max(x, axis=0, keepdims=True)
    e = jnp.exp(x - m)
    sm = e / jnp.sum(e, axis=0, keepdims=True)
    acc = jnp.zeros_like(sm[0:1])
    for i in range(n):
        if i > 0:
            acc = acc + sm[i:i + 1]
        o_ref[i:i + 1, :] = acc


def _lower_bounds(logits):
    return pl.pallas_call(
        _lb_kernel,
        out_shape=jax.ShapeDtypeStruct(logits.shape, F32),
        name="hgrn_lower_bounds",
    )(logits)


def _even_kernel(res_ref, ada_ref, preg_ref, postg_ref, win_ref, wout_ref, poolw_ref, pscale_ref,
                 lb_ref, og_ref, out_ref,
                 pbuf, st_ref, hq_ref, hk_ref, hc_ref, hv_ref, ho_ref):
    ti = pl.program_id(1)
    tt, d = res_ref.shape[1], res_ref.shape[2]
    pw = pbuf.shape[1]
    nh = pw // HGRN_HEAD
    sub = MAX_WINDOW
    pad = MAX_WINDOW

    @pl.when(ti == 0)
    def _():
        pbuf[0:pad, :] = jnp.zeros((pad, pw), F32)
        st_ref[...] = jnp.zeros(st_ref.shape, F32)

    x = res_ref[0]
    ada = ada_ref[0]
    shift, scale, gate = ada[:, :d], ada[:, d:2 * d], ada[:, 2 * d:]
    h = _rms(x, preg_ref[...]) * (1.0 + scale) + shift
    u = _dot(h, win_ref[...])

    row = _iota((tt, 1), 0)
    up = u[:, :pw]
    pbuf[pad:pad + tt, :] = up
    pos1 = (ti * tt + row + 1).astype(F32)
    gw = pw // len(POOL_WINDOWS)
    pooled = []
    for gi, win in enumerate(POOL_WINDOWS):
        sl = slice(gi * gw, (gi + 1) * gw)
        s = up[:, sl]
        for i in range(1, win):
            s = s + pbuf[pad - i:pad - i + tt, sl]
        p = s / jnp.minimum(pos1, float(win)) - up[:, sl]
        pooled.append(_dot(p, poolw_ref[gi]))
    pbuf[0:pad, :] = pbuf[tt:tt + pad, :]
    y_pool = jnp.concatenate(pooled, axis=-1) * pscale_ref[...]

    q = _silu(u[:, pw:2 * pw])
    fr = u[:, 2 * pw:3 * pw]
    v = u[:, 3 * pw:4 * pw]
    z = u[:, 4 * pw:]
    lb = lb_ref[...]
    k = (1.0 - lb) * _sigmoid(-fr)
    la = jnp.log(lb)
    lsig = jnp.minimum(fr, 0.0) - jnp.log1p(jnp.exp(-jnp.abs(fr)))
    lbb = jnp.log1p(-lb) + lsig
    logf = jnp.maximum(la, lbb) + jnp.log1p(jnp.exp(-jnp.abs(la - lbb)))

    r2 = _iota((tt, tt), 0)
    c2 = _iota((tt, tt), 1)
    same = (r2 // sub) == (c2 // sub)
    tri = jnp.where(same & (c2 <= r2), 1.0, 0.0).astype(BF16)
    blk = jnp.where(same, 1.0, 0.0).astype(BF16)
    cs = _dot_sel(tri, logf)
    csl = _dot_sel(blk, logf)
    qt = q * jnp.exp(cs)
    kvec = k * jnp.exp(csl - cs)
    dl = jnp.exp(csl)
    nsub = tt // sub
    heads = [slice(hh * HGRN_HEAD, (hh + 1) * HGRN_HEAD) for hh in range(nh)]
    subs = [slice(m * sub, (m + 1) * sub) for m in range(nsub)]

    upd = [[_dot_ta(v[rows, hs], kvec[rows, hs]) for hs in heads] for rows in subs]
    start = []
    states = [st_ref[hh] for hh in range(nh)]
    for m, rows in enumerate(subs):
        start.append([s.astype(BF16) for s in states])
        states = [s * dl[m * sub:m * sub + 1, hs] + upd[m][hh]
                  for hh, (s, hs) in enumerate(zip(states, heads))]
    for hh in range(nh):
        st_ref[hh] = states[hh]
    o_inter = [jnp.concatenate([_dot_tb(qt[rows, hs], start[m][hh]) for m, rows in enumerate(subs)],
                               axis=0) for hh, hs in enumerate(heads)]

    for hh, hs in enumerate(heads):
        hq_ref[hh] = q[:, hs]
        hk_ref[hh] = k[:, hs]
        hc_ref[hh] = cs[:, hs]
        hv_ref[hh] = v[:, hs]
    o_heads = []
    for hh in range(nh):
        at = lambda ref, t: ref[hh, pl.ds(t, nsub, stride=sub), :]
        qs = [at(hq_ref, t) for t in range(sub)]
        ks = [at(hk_ref, t) for t in range(sub)]
        cc = [at(hc_ref, t) for t in range(sub)]
        vv = [at(hv_ref, t) for t in range(sub)]
        for t in range(sub):
            acc = jnp.sum(qs[t] * ks[t], axis=-1, keepdims=True) * vv[t]
            for s in range(t):
                p = qs[t] * ks[s] * jnp.exp(cc[t] - cc[s])
                acc = acc + jnp.sum(p, axis=-1, keepdims=True) * vv[s]
            ho_ref[hh, pl.ds(t, nsub, stride=sub), :] = acc
        o_heads.append(o_inter[hh] + ho_ref[hh])
    og = og_ref[...]
    y_h = [o * lax.rsqrt(jnp.mean(o * o, axis=-1, keepdims=True) + NORM_EPS)
           * og[:, hh * HGRN_HEAD:(hh + 1) * HGRN_HEAD] for hh, o in enumerate(o_heads)]

    y = jnp.concatenate([y_pool] + y_h, axis=-1) * _silu(z)
    out = _dot(y, wout_ref[...])
    out_ref[0] = x + gate * _rms(out, postg_ref[...])


def _even_layer(res, ada, pre_g, post_g, w_in, w_out, pool_w, pool_scale, lb, onorm_g):
    bsz, t, d = res.shape
    tt = min(ROW_TILE, t)
    pw = pool_scale.shape[-1]
    nh = pw // HGRN_HEAD
    row_spec = pl.BlockSpec((1, tt, d), lambda b, i: (b, i, 0))
    return pl.pallas_call(
        _even_kernel,
        grid=(bsz, t // tt),
        in_specs=[row_spec,
                  pl.BlockSpec((1, 1, 3 * d), lambda b, i: (b, 0, 0)),
                  _full_spec((1, d)), _full_spec((1, d)),
                  _full_spec(w_in.shape), _full_spec(w_out.shape), _full_spec(pool_w.shape),
                  _full_spec((1, pw)), _full_spec((1, pw)), _full_spec((1, pw))],
        out_specs=row_spec,
        out_shape=jax.ShapeDtypeStruct(res.shape, F32),
        scratch_shapes=[pltpu.VMEM((MAX_WINDOW + tt, pw), F32),
                        pltpu.VMEM((nh, HGRN_HEAD, HGRN_HEAD), F32)]
        + [pltpu.VMEM((nh, tt, HGRN_HEAD), F32) for _ in range(5)],
        compiler_params=_params(),
        name="even_layer",
    )(res, ada.reshape(bsz, 1, 3 * d), pre_g.reshape(1, d), post_g.reshape(1, d),
      w_in.astype(BF16), w_out.astype(BF16), pool_w.astype(BF16),
      pool_scale.reshape(1, pw), lb.reshape(1, pw), onorm_g.reshape(1, pw))


def _head_reduce_matrix(d):
    return jnp.where(_iota((d, HEAD_SLOTS), 0) // RWKV_HEAD == _iota((d, HEAD_SLOTS), 1), 1.0, 0.0).astype(BF16)


def _head_spread_matrix(d):
    return jnp.where(_iota((HEAD_SLOTS, d), 0) == _iota((HEAD_SLOTS, d), 1) // RWKV_HEAD, 1.0, 0.0).astype(BF16)


def _odd_pre_kernel(has_vfirst, *refs):
    if has_vfirst:
        (res_ref, ada_ref, preg_ref, mu_ref, w_ref, w0_ref, w1_ref, w2_ref, a0_ref, a1_ref, a2_ref,
         kk_ref, ka_ref, vf_ref, v0_ref, v1_ref, v2_ref,
         r_out, lw_out, k_out, v_out, kk_out, b_out, z_out, prev_ref) = refs
    else:
        (res_ref, ada_ref, preg_ref, mu_ref, w_ref, w0_ref, w1_ref, w2_ref, a0_ref, a1_ref, a2_ref,
         kk_ref, ka_ref,
         r_out, lw_out, k_out, v_out, kk_out, b_out, z_out, prev_ref) = refs
    ti = pl.program_id(1)
    tt, d = res_ref.shape[1], res_ref.shape[2]

    @pl.when(ti == 0)
    def _():
        prev_ref[...] = jnp.zeros(prev_ref.shape, F32)

    x = res_ref[0]
    ada = ada_ref[0]
    shift, scale = ada[:, :d], ada[:, d:2 * d]
    h = _rms(x, preg_ref[...]) * (1.0 + scale) + shift
    row = _iota((tt, 1), 0)
    hs = jnp.where(row == 0, prev_ref[0:1, :], pltpu.roll(h, 1, axis=0))
    prev_ref[0:1, :] = h[tt - 1:tt, :]
    xx = hs - h
    xm = lambda p: h + xx * mu_ref[p:p + 1, :]

    r = _dot(xm(0), w_ref[0])
    k = _dot(xm(1), w_ref[1])
    xv = xm(2)
    v = _dot(xv, w_ref[2])
    z_out[0] = _dot(xm(3), w_ref[3])
    logw = -_softplus(-(w0_ref[...] + _dot(jnp.tanh(_dot(xm(4), w1_ref[...])), w2_ref[...]))) - 0.5
    lw_out[0] = -jnp.exp(logw)
    a = _sigmoid(a0_ref[...] + _dot(_dot(xm(5), a1_ref[...]), a2_ref[...]))
    if has_vfirst:
        v = v + (vf_ref[0] - v) * _sigmoid(v0_ref[...] + _dot(_dot(xv, v1_ref[...]), v2_ref[...]))
    kkr = k * kk_ref[...]
    ss = _dot(kkr * kkr, _head_reduce_matrix(d))
    kk = kkr * _dot_sel_r(1.0 / jnp.maximum(jnp.sqrt(ss), 1e-12), _head_spread_matrix(d))
    r_out[0] = r
    k_out[0] = k * (1.0 + (a - 1.0) * ka_ref[...])
    v_out[0] = v
    kk_out[0] = kk
    b_out[0] = kk * a


def _odd_pre(res, ada, pre_g, mu, w_rkvz, w0, w1, w2, a0, a1, a2, k_k, k_a, vfirst=None):
    bsz, t, d = res.shape
    tt = min(ROW_TILE, t)
    row_spec = pl.BlockSpec((1, tt, d), lambda b, i: (b, i, 0))
    vec = lambda a: a.reshape(1, d)
    args = [res, ada.reshape(bsz, 1, 3 * d), vec(pre_g), mu, w_rkvz.astype(BF16), vec(w0),
            w1.astype(BF16), w2.astype(BF16), vec(a0), a1.astype(BF16), a2.astype(BF16),
            vec(k_k), vec(k_a)]
    specs = [row_spec, pl.BlockSpec((1, 1, 3 * d), lambda b, i: (b, 0, 0)), _full_spec((1, d)),
             _full_spec(mu.shape), _full_spec(w_rkvz.shape), _full_spec((1, d)),
             _full_spec(w1.shape), _full_spec(w2.shape), _full_spec((1, d)),
             _full_spec(a1.shape), _full_spec(a2.shape), _full_spec((1, d)), _full_spec((1, d))]
    if vfirst is not None:
        v_first, v0, v1, v2 = vfirst
        args += [v_first, vec(v0), v1.astype(BF16), v2.astype(BF16)]
        specs += [row_spec, _full_spec((1, d)), _full_spec(v1.shape), _full_spec(v2.shape)]
    return pl.pallas_call(
        functools.partial(_odd_pre_kernel, vfirst is not None),
        grid=(bsz, t // tt),
        in_specs=specs,
        out_specs=[row_spec] * 7,
        out_shape=[jax.ShapeDtypeStruct(res.shape, F32)] * 7,
        scratch_shapes=[pltpu.VMEM((8, d), F32)],
        compiler_params=_params(),
        name="rwkv_pre",
    )(*args)


def _block_rows(y, n_heads, head):
    yb = y.astype(BF16)
    lane_head = _iota(yb.shape, 1) // head
    return jnp.concatenate([jnp.where(lane_head == hh, yb, jnp.zeros_like(yb)) for hh in range(n_heads)],
                           axis=0)


def _stack_heads(x, n_heads):
    w = x.shape[1] // n_heads
    return jnp.concatenate([x[:, hh * w:(hh + 1) * w] for hh in range(n_heads)], axis=0)


def _chunk_cumsum(x):
    tile = 8
    sub = _iota(x.shape, 0) % tile
    sh = 1
    while sh < tile:
        x = x + jnp.where(sub >= sh, pltpu.roll(x, sh, axis=0), 0.0)
        sh *= 2
    out, carry = [], None
    for j in range(x.shape[0] // tile):
        blk = x[j * tile:(j + 1) * tile, :]
        out.append(blk if carry is None else blk + carry)
        last = x[(j + 1) * tile - 1:(j + 1) * tile, :]
        carry = last if carry is None else carry + last
    return jnp.concatenate(out, axis=0)


def _scan_local_steps(ins, store):
    rs, lws, ks, vs, kks, bs = ins
    c_len, g = rs[0].shape
    nh = g // RWKV_HEAD
    hd = RWKV_HEAD
    each = lambda f, *ls: [f(*xs) for xs in zip(*ls)]
    bd = lambda y: _block_rows(y, nh, hd)
    mm = lambda x, y: _dot(x, bd(y))
    stack = lambda x, y: jnp.concatenate([x.astype(BF16), y.astype(BF16)], axis=0)
    wc = nh * c_len

    t_idx = _iota((c_len, wc), 0)
    s_idx = _iota((c_len, wc), 1) % c_len
    strict = s_idx < t_idx
    incl = s_idx <= t_idx
    diag_blk = strict & ((s_idx // RWKV_SUB) == (t_idx // RWKV_SUB))
    eye = jnp.where(s_idx == t_idx, 1.0, 0.0)
    dia = _iota((c_len, hd), 0) == _iota((c_len, hd), 1)

    e = {}
    steps = []

    def step(f):
        steps.append(f)
        return f

    @step
    def _():
        e["cs"] = each(_chunk_cumsum, lws)

    @step
    def _():
        cs = e["cs"]
        cl = [c[c_len - 1:c_len, :] for c in cs]
        e["rt"] = each(lambda r, c: r * jnp.exp(c), rs, cs)
        e["kkt"] = each(lambda kk, c, lw: kk * jnp.exp(c - lw), kks, cs, lws)
        e["kh"] = each(lambda k, c: k * jnp.exp(-c), ks, cs)
        e["bh"] = each(lambda b, c: b * jnp.exp(-c), bs, cs)
        e["pend"] = [jnp.exp(l) for l in cl]
        e["kvec"] = each(lambda kh, p: kh * p, e["kh"], e["pend"])
        e["bvec"] = each(lambda bh, p: bh * p, e["bh"], e["pend"])

    @step
    def _():
        a_all = each(lambda kkt, rt, bh, kh: _dot_tb(stack(kkt, rt), stack(bd(bh), bd(kh))),
                     e["kkt"], e["rt"], e["bh"], e["kh"])
        a_kb = [a[:c_len, :wc] for a in a_all]
        e["a_kk"] = [jnp.where(strict, a[:c_len, wc:], 0.0) for a in a_all]
        e["a_rb"] = [jnp.where(incl, a[c_len:, :wc], 0.0) for a in a_all]
        e["a_rk"] = [jnp.where(incl, a[c_len:, wc:], 0.0) for a in a_all]
        e["n_p"] = each(lambda a: jnp.where(diag_blk, a, 0.0), a_kb)
        e["n_o"] = each(lambda a: jnp.where(strict & ~diag_blk, a, 0.0), a_kb)
        e["t_d"] = each(lambda n: eye - n, e["n_p"])

    @step
    def _():
        e["n_p"] = each(mm, e["n_p"], e["n_p"])

    for _ in range(RWKV_SUB.bit_length() - 3):
        @step
        def _():
            both = each(lambda t, n: _dot(stack(t, n), bd(n)), e["t_d"], e["n_p"])
            e["t_d"] = each(lambda t, s: t + s[:c_len], e["t_d"], both)
            e["n_p"] = [s[c_len:] for s in both]

    @step
    def _():
        e["t_d"] = each(lambda t, n: t + mm(t, n), e["t_d"], e["n_p"])

    @step
    def _():
        e["z_p"] = each(mm, e["t_d"], e["n_o"])
        e["t_m"] = each(lambda z: eye - z, e["z_p"])

    for _ in range((c_len // RWKV_SUB).bit_length() - 2):
        @step
        def _():
            e["z_p"] = each(mm, e["z_p"], e["z_p"])

        @step
        def _():
            e["t_m"] = each(lambda t, z: t + mm(t, z), e["t_m"], e["z_p"])

    @step
    def _():
        e["tmat"] = each(mm, e["t_m"], e["t_d"])

    @step
    def _():
        e["av"] = each(lambda akk, ark, v: _dot(stack(akk, ark), bd(v)), e["a_kk"], e["a_rk"], vs)

    @step
    def _():
        e["kv"] = each(lambda v, kv: _dot_ta(_stack_heads(kv, nh), bd(v)), vs, e["kvec"])

    @step
    def _():
        for i in range(len(rs)):
            pcat = jnp.concatenate(
                [jnp.broadcast_to(jnp.sum(jnp.where(dia, e["pend"][i][:, hh * hd:(hh + 1) * hd], 0.0),
                                          axis=1, keepdims=True), (c_len, hd)) for hh in range(nh)], axis=1)
            store(i, kr=stack(e["kkt"][i], e["rt"][i]), tmat=e["tmat"][i].astype(BF16),
                  av=e["av"][i][:c_len], arkv=e["av"][i][c_len:], arb=e["a_rb"][i].astype(BF16),
                  bst=_stack_heads(e["bvec"][i].astype(BF16), nh), kv=e["kv"][i], pcat=pcat)

    return steps


def _scan_chain_steps(load, n_inst, states, emit):
    g = states[0].shape[1]
    nh = g // RWKV_HEAD
    c_len = RWKV_CHUNK
    bd = lambda y: _block_rows(y, nh, RWKV_HEAD)
    e = {}

    def s1():
        e["x"] = [_dot(load(i, "kr"), bd(states[i])) for i in range(n_inst)]

    def s2():
        e["u"] = [_dot(load(i, "tmat"), bd(e["x"][i][:c_len] + load(i, "av"))) for i in range(n_inst)]

    def s3():
        emit([e["x"][i][c_len:] + load(i, "arkv") - _dot(load(i, "arb"), bd(e["u"][i]))
              for i in range(n_inst)])

    def s4():
        for i in range(n_inst):
            states[i] = (states[i] * load(i, "pcat") + load(i, "kv")
                         - _dot_ta(load(i, "bst"), bd(e["u"][i])))

    return [s1, s2, s3, s4]


_SCAN_FIELDS = ("kr", "tmat", "av", "arkv", "arb", "bst", "kv", "pcat")


def _scan_kernel(r_ref, lw_ref, k_ref, v_ref, kk_ref, b_ref, y_ref, mt_ref, *bufs):
    ti = pl.program_id(1)
    tt, d = r_ref.shape[1], r_ref.shape[2]
    ng = d // GROUP_LANES
    nchunk = tt // RWKV_CHUNK
    ninst = nchunk * ng
    in_refs = (r_ref, lw_ref, k_ref, v_ref, kk_ref, b_ref)
    buf = dict(zip(_SCAN_FIELDS, bufs))
    wslot = (ti % 2) * ninst
    rslot = ((ti + 1) % 2) * ninst

    @pl.when(ti == 0)
    def _():
        mt_ref[...] = jnp.zeros(mt_ref.shape, F32)
        for name in _SCAN_FIELDS:
            ref = buf[name]
            ref[pl.ds(ninst, ninst)] = jnp.zeros((ninst,) + ref.shape[1:], ref.dtype)

    def store(i, **fields):
        for name, val in fields.items():
            buf[name][wslot + i] = val

    states = [mt_ref[gi] for gi in range(ng)]
    ys = []
    blocks, chain = [], []
    per_block = SCAN_CHUNKS_PER_STEP * ng
    for b0 in range(0, ninst, per_block):
        idx = range(b0, b0 + per_block)
        ins = [[ref[0, (i // ng) * RWKV_CHUNK:(i // ng + 1) * RWKV_CHUNK,
                    (i % ng) * GROUP_LANES:(i % ng + 1) * GROUP_LANES] for i in idx]
               for ref in in_refs]
        blocks.append(_scan_local_steps(ins, lambda i, _b0=b0, **f: store(_b0 + i, **f)))
    local = []
    for bi, steps in enumerate(blocks):
        at = min(len(local), bi * SCAN_BLOCK_STAGGER)
        merged = local[:at]
        rest = local[at:]
        for k in range(max(len(rest), len(steps))):
            merged += rest[k:k + 1] + steps[k:k + 1]
        local = merged
    for ci in range(nchunk):
        chain += _scan_chain_steps(lambda i, name, _c=ci: buf[name][rslot + _c * ng + i], ng,
                                   states, ys.append)
    done = 0
    span = max(1, (3 * len(local)) // 4)
    for j, f in enumerate(local):
        f()
        want = min(len(chain), -((j + 1) * len(chain) // -span))
        while done < want:
            chain[done]()
            done += 1
    for ci in range(nchunk):
        y_ref[0, ci * RWKV_CHUNK:(ci + 1) * RWKV_CHUNK, :] = jnp.concatenate(ys[ci], axis=-1)
    for gi in range(ng):
        mt_ref[gi] = states[gi]


def _rwkv_scan(r, lw, k, v, kk, b):
    bsz, t, d = r.shape
    tt = min(ROW_TILE, t)
    nt = t // tt
    ng = d // GROUP_LANES
    slots = 2 * (tt // RWKV_CHUNK) * ng
    c, g = RWKV_CHUNK, GROUP_LANES
    in_spec = pl.BlockSpec((1, tt, d), lambda bi, i: (bi, jnp.minimum(i, nt - 1), 0))
    out_spec = pl.BlockSpec((1, tt, d), lambda bi, i: (bi, jnp.maximum(i - 1, 0), 0))
    shapes = dict(kr=((2 * c, g), BF16), tmat=((c, g), BF16), av=((c, g), F32), arkv=((c, g), F32),
                  arb=((c, g), BF16), bst=((g, RWKV_HEAD), BF16), kv=((RWKV_HEAD, g), F32),
                  pcat=((RWKV_HEAD, g), F32))
    return pl.pallas_call(
        _scan_kernel,
        grid=(bsz, nt + 1),
        in_specs=[in_spec] * 6,
        out_specs=out_spec,
        out_shape=jax.ShapeDtypeStruct(r.shape, F32),
        scratch_shapes=[pltpu.VMEM((ng, RWKV_HEAD, g), F32)]
        + [pltpu.VMEM((slots,) + shapes[n][0], shapes[n][1]) for n in _SCAN_FIELDS],
        compiler_params=_params(),
        name="rwkv_scan",
    )(r, lw, k, v, kk, b)


def _odd_post_kernel(res_ref, ada_ref, y_ref, r_ref, k_ref, v_ref, z_ref, rk_ref, lg_ref, lb_ref,
                     postg_ref, wout_ref, out_ref):
    d = res_ref.shape[2]
    tt = res_ref.shape[1]
    red, spread = _head_reduce_matrix(d), _head_spread_matrix(d)
    inv = 1.0 / RWKV_HEAD
    y = y_ref[0]
    rkr = r_ref[0] * k_ref[0] * rk_ref[...]
    sums = _dot(jnp.concatenate([y, rkr], axis=0), red)
    both = _dot_sel_r(jnp.concatenate([sums[:tt] * inv, sums[tt:]], axis=0), spread)
    yc = y - both[:tt]
    rstd = lax.rsqrt(_dot(yc * yc, red) * inv + LNX_EPS)
    yn = yc * _dot_sel_r(rstd, spread) * lg_ref[...] + lb_ref[...]
    out = _dot((yn + both[tt:] * v_ref[0]) * _silu(z_ref[0]), wout_ref[...])
    gate = ada_ref[0][:, 2 * d:]
    out_ref[0] = res_ref[0] + gate * _rms(out, postg_ref[...])


def _odd_post(res, ada, y, r, k, v, z, r_k, lnx_g, lnx_b, post_g, w_out):
    bsz, t, d = res.shape
    tt = min(ROW_TILE, t)
    row_spec = pl.BlockSpec((1, tt, d), lambda b, i: (b, i, 0))
    vec = lambda a: a.reshape(1, d)
    return pl.pallas_call(
        _odd_post_kernel,
        grid=(bsz, t // tt),
        in_specs=[row_spec, pl.BlockSpec((1, 1, 3 * d), lambda b, i: (b, 0, 0))] + [row_spec] * 5
        + [_full_spec((1, d))] * 4 + [_full_spec(w_out.shape)],
        out_specs=row_spec,
        out_shape=jax.ShapeDtypeStruct(res.shape, F32),
        compiler_params=_params(),
        name="rwkv_post",
    )(res, ada.reshape(bsz, 1, 3 * d), y, r, k, v, z, vec(r_k), vec(lnx_g), vec(lnx_b), vec(post_g),
      w_out.astype(BF16))


def kernel(x, c, ada_w, ada_b, pre_g, post_g, ev_w_in, ev_w_out, pool_w, pool_scale, hgrn_lb_logits,
           hgrn_onorm_g, rw_mu, rw_w_rkvz, rw_w0, rw_w1, rw_w2, rw_a0, rw_a1, rw_a2, rw_k_k, rw_k_a,
           rw_r_k, rw_lnx_g, rw_lnx_b, rw_w_out, rw_v0, rw_v1, rw_v2):
    depth = ada_w.shape[0]
    res = x.astype(F32)
    ada = _ada_all(c, ada_w, ada_b)
    lb_all = _lower_bounds(hgrn_lb_logits)
    v_first = None
    for layer in range(depth):
        j = layer // 2
        if layer % 2 == 0:
            res = _even_layer(res, ada[layer], pre_g[layer], post_g[layer], ev_w_in[j], ev_w_out[j],
                              pool_w[j], pool_scale[j], lb_all[j], hgrn_onorm_g[j])
        else:
            vfirst = None if v_first is None else (v_first, rw_v0[j - 1], rw_v1[j - 1], rw_v2[j - 1])
            r, lw, k, v, kk, b, z = _odd_pre(res, ada[layer], pre_g[layer], rw_mu[j], rw_w_rkvz[j],
                                             rw_w0[j], rw_w1[j], rw_w2[j], rw_a0[j], rw_a1[j],
                                             rw_a2[j], rw_k_k[j], rw_k_a[j], vfirst)
            if v_first is None:
                v_first = v
            y = _rwkv_scan(r, lw, k, v, kk, b)
            res = _odd_post(res, ada[layer], y, r, k, v, z, rw_r_k[j], rw_lnx_g[j], rw_lnx_b[j],
                            post_g[layer], rw_w_out[j])
    return res.astype(x.dtype)
```

```python
import functools

import jax
import jax.numpy as jnp
from jax import lax
from jax.experimental import pallas as pl
from jax.experimental.pallas import tpu as pltpu

F32 = jnp.float32
BF16 = jnp.bfloat16

NORM_EPS = 1e-6
LOG2_E = 1.4426950408889634
LNX_EPS = 64e-5
POOL_WINDOWS = (2, 4, 8, 16)
MAX_WINDOW = 16
HGRN_HEAD = 128
RWKV_HEAD = 64
RWKV_CHUNK = 64
RWKV_SUB = 16
HEAD_SLOTS = 128
GROUP_LANES = 256
SCAN_CHUNKS_PER_STEP = 2
SCAN_BLOCK_STAGGER = 5
ROW_TILE = 256
WIDE_ROW_TILE = 512
VMEM_LIMIT_BYTES = 56 * 1024 * 1024


def _dot(a, b):
    return jnp.dot(a.astype(BF16), b.astype(BF16), preferred_element_type=F32)


def _dot_tb(a, b):
    return lax.dot_general(a.astype(BF16), b.astype(BF16), (((1,), (1,)), ((), ())),
                           preferred_element_type=F32)


def _dot_ta(a, b):
    return lax.dot_general(a.astype(BF16), b.astype(BF16), (((0,), (0,)), ((), ())),
                           preferred_element_type=F32)


def _dot_sel_r(x, sel):
    hi = x.astype(BF16)
    lo = (x - hi.astype(F32)).astype(BF16)
    f = lambda p: jnp.dot(p, sel, preferred_element_type=F32)
    return f(lo) + f(hi)


def _sigmoid(x):
    return jax.nn.sigmoid(x)


def _silu(x):
    return x * _sigmoid(x)


def _log1pexp_neg_abs(x):
    return jnp.log(1.0 + jnp.exp(-jnp.abs(x)))


def _softplus(x):
    return jnp.maximum(x, 0.0) + _log1pexp_neg_abs(x)


def _rms(x, g):
    return x * lax.rsqrt(jnp.mean(x * x, axis=-1, keepdims=True) + NORM_EPS) * g


def _iota(shape, dim):
    return lax.broadcasted_iota(jnp.int32, shape, dim)


def _segment_cumsum(x, seg):
    tile = 8
    sub = _iota(x.shape, 0) % tile
    sh = 1
    while sh < tile:
        x = x + jnp.where(sub >= sh, pltpu.roll(x, sh, axis=0), 0.0)
        sh *= 2
    per = seg // tile
    out, last, carry = [], [], None
    for j in range(x.shape[0] // tile):
        blk = x[j * tile:(j + 1) * tile, :]
        tot = blk[tile - 1:tile, :]
        if j % per:
            blk, tot = blk + carry, tot + carry
        out.append(blk)
        carry = tot
        if j % per == per - 1:
            last += [jnp.broadcast_to(tot, (seg, x.shape[1]))]
    return jnp.concatenate(out, axis=0), jnp.concatenate(last, axis=0)


def _full_spec(shape):
    nd = len(shape)
    return pl.BlockSpec(shape, lambda *_: (0,) * nd)


def _params(**flags):
    return pltpu.CompilerParams(dimension_semantics=("arbitrary", "arbitrary"),
                                vmem_limit_bytes=VMEM_LIMIT_BYTES, flags=flags or None)


def _ada_kernel(c_ref, w_ref, b_ref, o_ref):
    cond = _silu(c_ref[...].astype(F32))
    o_ref[0] = jnp.dot(cond, w_ref[0].astype(F32), preferred_element_type=F32,
                       precision=lax.Precision.HIGHEST) + b_ref[0]


def _ada_all(c, ada_w, ada_b):
    depth, d, d3 = ada_w.shape
    bsz = c.shape[0]
    tn = 1024
    return pl.pallas_call(
        _ada_kernel,
        grid=(depth, d3 // tn),
        in_specs=[pl.BlockSpec((bsz, d), lambda l, n: (0, 0)),
                  pl.BlockSpec((1, d, tn), lambda l, n: (l, 0, n)),
                  pl.BlockSpec((1, 1, tn), lambda l, n: (l, 0, n))],
        out_specs=pl.BlockSpec((1, bsz, tn), lambda l, n: (l, 0, n)),
        out_shape=jax.ShapeDtypeStruct((depth, bsz, d3), F32),
        compiler_params=_params(),
        name="ada_ln",
    )(c, ada_w, ada_b.reshape(depth, 1, d3))


def _lb_kernel(logit_ref, o_ref):
    x = logit_ref[...].astype(F32)
    n = x.shape[0]
    m = jnp.max(x, axis=0, keepdims=True)
    e = jnp.exp(x - m)
    sm = e / jnp.sum(e, axis=0, keepdims=True)
    acc = jnp.zeros_like(sm[0:1])
    for i in range(n):
        if i > 0:
            acc = acc + sm[i:i + 1]
        o_ref[i:i + 1, :] = acc


def _lower_bounds(logits):
    return pl.pallas_call(
        _lb_kernel,
        out_shape=jax.ShapeDtypeStruct(logits.shape, F32),
        name="hgrn_lower_bounds",
    )(logits)


def _even_kernel(res_ref, ada_ref, preg_ref, postg_ref, win_ref, wout_ref, poolw_ref, pscale_ref,
                 lb_ref, og_ref, out_ref,
                 pbuf, st_ref, hq_ref, hk_ref, hc_ref, hv_ref, ho_ref):
    ti = pl.program_id(1)
    tt, d = res_ref.shape[1], res_ref.shape[2]
    pw = pbuf.shape[1]
    nh = pw // HGRN_HEAD
    sub = MAX_WINDOW
    pad = MAX_WINDOW

    @pl.when(ti == 0)
    def _():
        pbuf[0:pad, :] = jnp.zeros((pad, pw), F32)
        st_ref[...] = jnp.zeros(st_ref.shape, F32)

    x = res_ref[0]
    ada = ada_ref[0]
    shift, scale, gate = ada[:, :d], ada[:, d:2 * d], ada[:, 2 * d:]
    h = _rms(x, preg_ref[...]) * (1.0 + scale) + shift
    u = _dot(h, win_ref[...])

    row = _iota((tt, 1), 0)
    up = u[:, :pw]
    pbuf[pad:pad + tt, :] = up
    pos1 = (ti * tt + row + 1).astype(F32)
    gw = pw // len(POOL_WINDOWS)
    pooled = []
    for gi, win in enumerate(POOL_WINDOWS):
        sl = slice(gi * gw, (gi + 1) * gw)
        s = pbuf[:, sl]
        sh = 1
        while sh < win:
            s = s + pltpu.roll(s, sh, axis=0)
            sh *= 2
        p = s[pad:, :] / jnp.minimum(pos1, float(win)) - up[:, sl]
        pooled.append(_dot(p, poolw_ref[gi]))
    pbuf[0:pad, :] = pbuf[tt:tt + pad, :]
    y_pool = jnp.concatenate(pooled, axis=-1) * pscale_ref[...]

    q = _silu(u[:, pw:2 * pw])
    fr = u[:, 2 * pw:3 * pw]
    v = u[:, 3 * pw:4 * pw]
    z = u[:, 4 * pw:]
    lb = lb_ref[...]
    k = (1.0 - lb) * _sigmoid(-fr)
    la = jnp.log(lb)
    lsig = jnp.minimum(fr, 0.0) - _log1pexp_neg_abs(fr)
    lbb = jnp.log1p(-lb) + lsig
    logf = jnp.maximum(la, lbb) + _log1pexp_neg_abs(la - lbb)

    cs, csl = _segment_cumsum(logf, sub)
    qt = q * jnp.exp(cs)
    kvec = k * jnp.exp(csl - cs)
    dl = jnp.exp(csl)
    nsub = tt // sub
    heads = [slice(hh * HGRN_HEAD, (hh + 1) * HGRN_HEAD) for hh in range(nh)]
    subs = [slice(m * sub, (m + 1) * sub) for m in range(nsub)]

    upd = [[_dot_ta(v[rows, hs], kvec[rows, hs]) for hs in heads] for rows in subs]
    start = []
    states = [st_ref[hh] for hh in range(nh)]
    for m, rows in enumerate(subs):
        start.append([s.astype(BF16) for s in states])
        states = [s * dl[m * sub:m * sub + 1, hs] + upd[m][hh]
                  for hh, (s, hs) in enumerate(zip(states, heads))]
    for hh in range(nh):
        st_ref[hh] = states[hh]
    o_inter = [jnp.concatenate([_dot_tb(qt[rows, hs], start[m][hh]) for m, rows in enumerate(subs)],
                               axis=0) for hh, hs in enumerate(heads)]

    for hh, hs in enumerate(heads):
        hq_ref[hh] = q[:, hs]
        hk_ref[hh] = k[:, hs]
        hc_ref[hh] = cs[:, hs] * LOG2_E
        hv_ref[hh] = v[:, hs]
    o_heads = []
    for hh in range(nh):
        at = lambda ref, t: ref[hh, pl.ds(t, nsub, stride=sub), :]
        qs = [at(hq_ref, t) for t in range(sub)]
        ks = [at(hk_ref, t) for t in range(sub)]
        cc = [at(hc_ref, t) for t in range(sub)]
        vv = [at(hv_ref, t) for t in range(sub)]
        for t in range(sub):
            acc = jnp.sum(qs[t] * ks[t], axis=-1, keepdims=True) * vv[t]
            for s in range(t):
                p = qs[t] * ks[s] * jnp.exp2(cc[t] - cc[s])
                acc = acc + jnp.sum(p, axis=-1, keepdims=True) * vv[s]
            ho_ref[hh, pl.ds(t, nsub, stride=sub), :] = acc
        o_heads.append(o_inter[hh] + ho_ref[hh])
    og = og_ref[...]
    y_h = [o * lax.rsqrt(jnp.mean(o * o, axis=-1, keepdims=True) + NORM_EPS)
           * og[:, hh * HGRN_HEAD:(hh + 1) * HGRN_HEAD] for hh, o in enumerate(o_heads)]

    y = jnp.concatenate([y_pool] + y_h, axis=-1) * _silu(z)
    out = _dot(y, wout_ref[...])
    out_ref[0] = x + gate * _rms(out, postg_ref[...])


def _even_layer(res, ada, pre_g, post_g, w_in, w_out, pool_w, pool_scale, lb, onorm_g):
    bsz, t, d = res.shape
    tt = min(ROW_TILE, t)
    pw = pool_scale.shape[-1]
    nh = pw // HGRN_HEAD
    row_spec = pl.BlockSpec((1, tt, d), lambda b, i: (b, i, 0))
    return pl.pallas_call(
        _even_kernel,
        grid=(bsz, t // tt),
        in_specs=[row_spec,
                  pl.BlockSpec((1, 1, 3 * d), lambda b, i: (b, 0, 0)),
                  _full_spec((1, d)), _full_spec((1, d)),
                  _full_spec(w_in.shape), _full_spec(w_out.shape), _full_spec(pool_w.shape),
                  _full_spec((1, pw)), _full_spec((1, pw)), _full_spec((1, pw))],
        out_specs=row_spec,
        out_shape=jax.ShapeDtypeStruct(res.shape, F32),
        scratch_shapes=[pltpu.VMEM((MAX_WINDOW + tt, pw), F32),
                        pltpu.VMEM((nh, HGRN_HEAD, HGRN_HEAD), F32)]
        + [pltpu.VMEM((nh, tt, HGRN_HEAD), F32) for _ in range(5)],
        compiler_params=_params(),
        name="even_layer",
    )(res, ada.reshape(bsz, 1, 3 * d), pre_g.reshape(1, d), post_g.reshape(1, d),
      w_in.astype(BF16), w_out.astype(BF16), pool_w.astype(BF16),
      pool_scale.reshape(1, pw), lb.reshape(1, pw), onorm_g.reshape(1, pw))


def _head_reduce_matrix(d):
    return jnp.where(_iota((d, HEAD_SLOTS), 0) // RWKV_HEAD == _iota((d, HEAD_SLOTS), 1), 1.0, 0.0).astype(BF16)


def _head_spread_matrix(d):
    return jnp.where(_iota((HEAD_SLOTS, d), 0) == _iota((HEAD_SLOTS, d), 1) // RWKV_HEAD, 1.0, 0.0).astype(BF16)


def _odd_pre_kernel(has_vfirst, *refs):
    if has_vfirst:
        (res_ref, ada_ref, preg_ref, mu_ref, w_ref, w0_ref, w1_ref, w2_ref, a0_ref, a1_ref, a2_ref,
         kk_ref, ka_ref, vf_ref, v0_ref, v1_ref, v2_ref,
         r_out, lw_out, k_out, v_out, kk_out, b_out, z_out, prev_ref) = refs
    else:
        (res_ref, ada_ref, preg_ref, mu_ref, w_ref, w0_ref, w1_ref, w2_ref, a0_ref, a1_ref, a2_ref,
         kk_ref, ka_ref,
         r_out, lw_out, k_out, v_out, kk_out, b_out, z_out, prev_ref) = refs
    ti = pl.program_id(1)
    tt, d = res_ref.shape[1], res_ref.shape[2]

    @pl.when(ti == 0)
    def _():
        prev_ref[...] = jnp.zeros(prev_ref.shape, F32)

    x = res_ref[0]
    ada = ada_ref[0]
    shift, scale = ada[:, :d], ada[:, d:2 * d]
    h = _rms(x, preg_ref[...]) * (1.0 + scale) + shift
    row = _iota((tt, 1), 0)
    hs = jnp.where(row == 0, prev_ref[0:1, :], pltpu.roll(h, 1, axis=0))
    prev_ref[0:1, :] = h[tt - 1:tt, :]
    xx = hs - h
    xm = lambda p: h + xx * mu_ref[p:p + 1, :]

    r = _dot(xm(0), w_ref[0])
    k = _dot(xm(1), w_ref[1])
    xv = xm(2)
    v = _dot(xv, w_ref[2])
    z_out[0] = _dot(xm(3), w_ref[3])
    logw = -_softplus(-(w0_ref[...] + _dot(jnp.tanh(_dot(xm(4), w1_ref[...])), w2_ref[...]))) - 0.5
    lw_out[0] = -jnp.exp(logw)
    a = _sigmoid(a0_ref[...] + _dot(_dot(xm(5), a1_ref[...]), a2_ref[...]))
    if has_vfirst:
        v = v + (vf_ref[0] - v) * _sigmoid(v0_ref[...] + _dot(_dot(xv, v1_ref[...]), v2_ref[...]))
    kkr = k * kk_ref[...]
    ss = _dot(kkr * kkr, _head_reduce_matrix(d))
    kk = kkr * _dot_sel_r(1.0 / jnp.maximum(jnp.sqrt(ss), 1e-12), _head_spread_matrix(d))
    r_out[0] = r
    k_out[0] = k * (1.0 + (a - 1.0) * ka_ref[...])
    v_out[0] = v
    kk_out[0] = kk
    b_out[0] = kk * a


def _odd_pre(res, ada, pre_g, mu, w_rkvz, w0, w1, w2, a0, a1, a2, k_k, k_a, vfirst=None):
    bsz, t, d = res.shape
    tt = min(ROW_TILE, t)
    row_spec = pl.BlockSpec((1, tt, d), lambda b, i: (b, i, 0))
    vec = lambda a: a.reshape(1, d)
    args = [res, ada.reshape(bsz, 1, 3 * d), vec(pre_g), mu, w_rkvz.astype(BF16), vec(w0),
            w1.astype(BF16), w2.astype(BF16), vec(a0), a1.astype(BF16), a2.astype(BF16),
            vec(k_k), vec(k_a)]
    specs = [row_spec, pl.BlockSpec((1, 1, 3 * d), lambda b, i: (b, 0, 0)), _full_spec((1, d)),
             _full_spec(mu.shape), _full_spec(w_rkvz.shape), _full_spec((1, d)),
             _full_spec(w1.shape), _full_spec(w2.shape), _full_spec((1, d)),
             _full_spec(a1.shape), _full_spec(a2.shape), _full_spec((1, d)), _full_spec((1, d))]
    if vfirst is not None:
        v_first, v0, v1, v2 = vfirst
        args += [v_first, vec(v0), v1.astype(BF16), v2.astype(BF16)]
        specs += [row_spec, _full_spec((1, d)), _full_spec(v1.shape), _full_spec(v2.shape)]
    return pl.pallas_call(
        functools.partial(_odd_pre_kernel, vfirst is not None),
        grid=(bsz, t // tt),
        in_specs=specs,
        out_specs=[row_spec] * 7,
        out_shape=[jax.ShapeDtypeStruct(res.shape, F32)] * 7,
        scratch_shapes=[pltpu.VMEM((8, d), F32)],
        compiler_params=_params(),
        name="rwkv_pre",
    )(*args)


def _block_rows(y, n_heads, head):
    yb = y.astype(BF16)
    lane_head = _iota(yb.shape, 1) // head
    return jnp.concatenate([jnp.where(lane_head == hh, yb, jnp.zeros_like(yb)) for hh in range(n_heads)],
                           axis=0)


def _stack_heads(x, n_heads):
    w = x.shape[1] // n_heads
    return jnp.concatenate([x[:, hh * w:(hh + 1) * w] for hh in range(n_heads)], axis=0)


def _scan_local_steps(ins, store):
    rs, lws, ks, vs, kks, bs = ins
    c_len, g = rs[0].shape
    nh = g // RWKV_HEAD
    hd = RWKV_HEAD
    each = lambda f, *ls: [f(*xs) for xs in zip(*ls)]
    bd = lambda y: _block_rows(y, nh, hd)
    mm = lambda x, y: _dot(x, bd(y))
    stack = lambda x, y: jnp.concatenate([x.astype(BF16), y.astype(BF16)], axis=0)
    wc = nh * c_len

    t_idx = _iota((c_len, wc), 0)
    s_idx = _iota((c_len, wc), 1) % c_len
    strict = s_idx < t_idx
    incl = s_idx <= t_idx
    diag_blk = strict & ((s_idx // RWKV_SUB) == (t_idx // RWKV_SUB))
    eye = jnp.where(s_idx == t_idx, 1.0, 0.0)
    dia = _iota((c_len, hd), 0) == _iota((c_len, hd), 1)

    e = {}
    steps = []

    def step(f):
        steps.append(f)
        return f

    @step
    def _():
        e["cs"] = each(lambda lw: _segment_cumsum(lw, c_len)[0], lws)

    @step
    def _():
        cs = e["cs"]
        cl = [c[c_len - 1:c_len, :] for c in cs]
        e["rt"] = each(lambda r, c: r * jnp.exp(c), rs, cs)
        e["kkt"] = each(lambda kk, c, lw: kk * jnp.exp(c - lw), kks, cs, lws)
        e["kh"] = each(lambda k, c: k * jnp.exp(-c), ks, cs)
        e["bh"] = each(lambda b, c: b * jnp.exp(-c), bs, cs)
        e["pend"] = [jnp.exp(l) for l in cl]
        e["kvec"] = each(lambda kh, p: kh * p, e["kh"], e["pend"])
        e["bvec"] = each(lambda bh, p: bh * p, e["bh"], e["pend"])

    @step
    def _():
        a_all = each(lambda kkt, rt, bh, kh: _dot_tb(stack(kkt, rt), stack(bd(bh), bd(kh))),
                     e["kkt"], e["rt"], e["bh"], e["kh"])
        a_kb = [a[:c_len, :wc] for a in a_all]
        e["a_kk"] = [jnp.where(strict, a[:c_len, wc:], 0.0) for a in a_all]
        e["a_rb"] = [jnp.where(incl, a[c_len:, :wc], 0.0) for a in a_all]
        e["a_rk"] = [jnp.where(incl, a[c_len:, wc:], 0.0) for a in a_all]
        e["n_p"] = each(lambda a: jnp.where(diag_blk, a, 0.0), a_kb)
        e["n_o"] = each(lambda a: jnp.where(strict & ~diag_blk, a, 0.0), a_kb)
        e["t_d"] = each(lambda n: eye - n, e["n_p"])

    @step
    def _():
        e["n_p"] = each(mm, e["n_p"], e["n_p"])

    for _ in range(RWKV_SUB.bit_length() - 3):
        @step
        def _():
            both = each(lambda t, n: _dot(stack(t, n), bd(n)), e["t_d"], e["n_p"])
            e["t_d"] = each(lambda t, s: t + s[:c_len], e["t_d"], both)
            e["n_p"] = [s[c_len:] for s in both]

    @step
    def _():
        e["t_d"] = each(lambda t, n: t + mm(t, n), e["t_d"], e["n_p"])

    @step
    def _():
        e["z_p"] = each(mm, e["t_d"], e["n_o"])
        e["t_m"] = each(lambda z: eye - z, e["z_p"])

    for _ in range((c_len // RWKV_SUB).bit_length() - 2):
        @step
        def _():
            e["z_p"] = each(mm, e["z_p"], e["z_p"])

        @step
        def _():
            e["t_m"] = each(lambda t, z: t + mm(t, z), e["t_m"], e["z_p"])

    @step
    def _():
        e["tmat"] = each(mm, e["t_m"], e["t_d"])

    @step
    def _():
        e["av"] = each(lambda akk, ark, v: _dot(stack(akk, ark), bd(v)), e["a_kk"], e["a_rk"], vs)

    @step
    def _():
        e["kv"] = each(lambda v, kv: _dot_ta(_stack_heads(kv, nh), bd(v)), vs, e["kvec"])

    @step
    def _():
        for i in range(len(rs)):
            pcat = jnp.concatenate(
                [jnp.broadcast_to(jnp.sum(jnp.where(dia, e["pend"][i][:, hh * hd:(hh + 1) * hd], 0.0),
                                          axis=1, keepdims=True), (c_len, hd)) for hh in range(nh)], axis=1)
            store(i, kr=stack(e["kkt"][i], e["rt"][i]), tmat=e["tmat"][i].astype(BF16),
                  av=e["av"][i][:c_len], arkv=e["av"][i][c_len:], arb=e["a_rb"][i].astype(BF16),
                  bst=_stack_heads(e["bvec"][i].astype(BF16), nh), kv=e["kv"][i], pcat=pcat)

    return steps


def _scan_chain_steps(load, n_inst, states, emit):
    g = states[0].shape[1]
    nh = g // RWKV_HEAD
    c_len = RWKV_CHUNK
    bd = lambda y: _block_rows(y, nh, RWKV_HEAD)
    e = {}

    def s1():
        e["x"] = [_dot(load(i, "kr"), bd(states[i])) for i in range(n_inst)]

    def s2():
        e["u"] = [_dot(load(i, "tmat"), bd(e["x"][i][:c_len] + load(i, "av"))) for i in range(n_inst)]

    def s3():
        emit([e["x"][i][c_len:] + load(i, "arkv") - _dot(load(i, "arb"), bd(e["u"][i]))
              for i in range(n_inst)])

    def s4():
        for i in range(n_inst):
            states[i] = (states[i] * load(i, "pcat") + load(i, "kv")
                         - _dot_ta(load(i, "bst"), bd(e["u"][i])))

    return [s1, s2, s3, s4]


_SCAN_FIELDS = ("kr", "tmat", "av", "arkv", "arb", "bst", "kv", "pcat")


def _scan_kernel(r_ref, lw_ref, k_ref, v_ref, kk_ref, b_ref, y_ref, mt_ref, *bufs):
    ti = pl.program_id(1)
    tt, d = r_ref.shape[1], r_ref.shape[2]
    ng = d // GROUP_LANES
    nchunk = tt // RWKV_CHUNK
    ninst = nchunk * ng
    in_refs = (r_ref, lw_ref, k_ref, v_ref, kk_ref, b_ref)
    buf = dict(zip(_SCAN_FIELDS, bufs))
    wslot = (ti % 2) * ninst
    rslot = ((ti + 1) % 2) * ninst

    @pl.when(ti == 0)
    def _():
        mt_ref[...] = jnp.zeros(mt_ref.shape, F32)
        for name in _SCAN_FIELDS:
            ref = buf[name]
            ref[pl.ds(ninst, ninst)] = jnp.zeros((ninst,) + ref.shape[1:], ref.dtype)

    def store(i, **fields):
        for name, val in fields.items():
            buf[name][wslot + i] = val

    states = [mt_ref[gi] for gi in range(ng)]
    ys = []
    blocks, chain = [], []
    per_block = SCAN_CHUNKS_PER_STEP * ng
    for b0 in range(0, ninst, per_block):
        idx = range(b0, b0 + per_block)
        ins = [[ref[0, (i // ng) * RWKV_CHUNK:(i // ng + 1) * RWKV_CHUNK,
                    (i % ng) * GROUP_LANES:(i % ng + 1) * GROUP_LANES] for i in idx]
               for ref in in_refs]
        blocks.append(_scan_local_steps(ins, lambda i, _b0=b0, **f: store(_b0 + i, **f)))
    local = []
    for bi, steps in enumerate(blocks):
        at = min(len(local), bi * SCAN_BLOCK_STAGGER)
        merged = local[:at]
        rest = local[at:]
        for k in range(max(len(rest), len(steps))):
            merged += rest[k:k + 1] + steps[k:k + 1]
        local = merged
    for ci in range(nchunk):
        chain += _scan_chain_steps(lambda i, name, _c=ci: buf[name][rslot + _c * ng + i], ng,
                                   states, ys.append)
    done = 0
    span = max(1, (3 * len(local)) // 4)
    for j, f in enumerate(local):
        f()
        want = min(len(chain), -((j + 1) * len(chain) // -span))
        while done < want:
            chain[done]()
            done += 1
    for ci in range(nchunk):
        y_ref[0, ci * RWKV_CHUNK:(ci + 1) * RWKV_CHUNK, :] = jnp.concatenate(ys[ci], axis=-1)
    for gi in range(ng):
        mt_ref[gi] = states[gi]


def _rwkv_scan(r, lw, k, v, kk, b):
    bsz, t, d = r.shape
    tt = min(ROW_TILE, t)
    nt = t // tt
    ng = d // GROUP_LANES
    slots = 2 * (tt // RWKV_CHUNK) * ng
    c, g = RWKV_CHUNK, GROUP_LANES
    in_spec = pl.BlockSpec((1, tt, d), lambda bi, i: (bi, jnp.minimum(i, nt - 1), 0))
    out_spec = pl.BlockSpec((1, tt, d), lambda bi, i: (bi, jnp.maximum(i - 1, 0), 0))
    shapes = dict(kr=((2 * c, g), BF16), tmat=((c, g), BF16), av=((c, g), F32), arkv=((c, g), F32),
                  arb=((c, g), BF16), bst=((g, RWKV_HEAD), BF16), kv=((RWKV_HEAD, g), F32),
                  pcat=((RWKV_HEAD, g), F32))
    return pl.pallas_call(
        _scan_kernel,
        grid=(bsz, nt + 1),
        in_specs=[in_spec] * 6,
        out_specs=out_spec,
        out_shape=jax.ShapeDtypeStruct(r.shape, F32),
        scratch_shapes=[pltpu.VMEM((ng, RWKV_HEAD, g), F32)]
        + [pltpu.VMEM((slots,) + shapes[n][0], shapes[n][1]) for n in _SCAN_FIELDS],
        compiler_params=_params(),
        name="rwkv_scan",
    )(r, lw, k, v, kk, b)


def _odd_post_kernel(res_ref, ada_ref, y_ref, r_ref, k_ref, v_ref, z_ref, rk_ref, lg_ref, lb_ref,
                     postg_ref, wout_ref, out_ref):
    d = res_ref.shape[2]
    tt = res_ref.shape[1]
    red, spread = _head_reduce_matrix(d), _head_spread_matrix(d)
    inv = 1.0 / RWKV_HEAD
    y = y_ref[0]
    rkr = r_ref[0] * k_ref[0] * rk_ref[...]
    sums = _dot(jnp.concatenate([y, rkr], axis=0), red)
    both = _dot_sel_r(jnp.concatenate([sums[:tt] * inv, sums[tt:]], axis=0), spread)
    yc = y - both[:tt]
    rstd = lax.rsqrt(_dot(yc * yc, red) * inv + LNX_EPS)
    yn = yc * _dot_sel_r(rstd, spread) * lg_ref[...] + lb_ref[...]
    out = _dot((yn + both[tt:] * v_ref[0]) * _silu(z_ref[0]), wout_ref[...])
    gate = ada_ref[0][:, 2 * d:]
    out_ref[0] = res_ref[0] + gate * _rms(out, postg_ref[...])


def _odd_post(res, ada, y, r, k, v, z, r_k, lnx_g, lnx_b, post_g, w_out):
    bsz, t, d = res.shape
    tt = min(WIDE_ROW_TILE, t)
    row_spec = pl.BlockSpec((1, tt, d), lambda b, i: (b, i, 0))
    vec = lambda a: a.reshape(1, d)
    return pl.pallas_call(
        _odd_post_kernel,
        grid=(bsz, t // tt),
        in_specs=[row_spec, pl.BlockSpec((1, 1, 3 * d), lambda b, i: (b, 0, 0))] + [row_spec] * 5
        + [_full_spec((1, d))] * 4 + [_full_spec(w_out.shape)],
        out_specs=row_spec,
        out_shape=jax.ShapeDtypeStruct(res.shape, F32),
        compiler_params=_params(),
        name="rwkv_post",
    )(res, ada.reshape(bsz, 1, 3 * d), y, r, k, v, z, vec(r_k), vec(lnx_g), vec(lnx_b), vec(post_g),
      w_out.astype(BF16))


def kernel(x, c, ada_w, ada_b, pre_g, post_g, ev_w_in, ev_w_out, pool_w, pool_scale, hgrn_lb_logits,
           hgrn_onorm_g, rw_mu, rw_w_rkvz, rw_w0, rw_w1, rw_w2, rw_a0, rw_a1, rw_a2, rw_k_k, rw_k_a,
           rw_r_k, rw_lnx_g, rw_lnx_b, rw_w_out, rw_v0, rw_v1, rw_v2):
    depth = ada_w.shape[0]
    res = x.astype(F32)
    ada = _ada_all(c, ada_w, ada_b)
    lb_all = _lower_bounds(hgrn_lb_logits)
    v_first = None
    for layer in range(depth):
        j = layer // 2
        if layer % 2 == 0:
            res = _even_layer(res, ada[layer], pre_g[layer], post_g[layer], ev_w_in[j], ev_w_out[j],
                              pool_w[j], pool_scale[j], lb_all[j], hgrn_onorm_g[j])
        else:
            vfirst = None if v_first is None else (v_first, rw_v0[j - 1], rw_v1[j - 1], rw_v2[j - 1])
            r, lw, k, v, kk, b, z = _odd_pre(res, ada[layer], pre_g[layer], rw_mu[j], rw_w_rkvz[j],
                                             rw_w0[j], rw_w1[j], rw_w2[j], rw_a0[j], rw_a1[j],
                                             rw_a2[j], rw_k_k[j], rw_k_a[j], vfirst)
            if v_first is None:
                v_first = v
            y = _rwkv_scan(r, lw, k, v, kk, b)
            res = _odd_post(res, ada[layer], y, r, k, v, z, rw_r_k[j], rw_lnx_g[j], rw_lnx_b[j],
                            post_g[layer], rw_w_out[j])
    return res.astype(x.dtype)
```

```python
import functools

import jax
import jax.numpy as jnp
from jax import lax
from jax.experimental import pallas as pl
from jax.experimental.pallas import tpu as pltpu

F32 = jnp.float32
BF16 = jnp.bfloat16

NORM_EPS = 1e-6
LOG2_E = 1.4426950408889634
LNX_EPS = 64e-5
POOL_WINDOWS = (2, 4, 8, 16)
MAX_WINDOW = 16
HGRN_HEAD = 128
RWKV_HEAD = 64
RWKV_CHUNK = 64
RWKV_SUB = 16
HEAD_SLOTS = 128
GROUP_LANES = 256
SCAN_CHUNKS_PER_STEP = 2
SCAN_BLOCK_STAGGER = 5
ROW_TILE = 256
WIDE_ROW_TILE = 512
VMEM_LIMIT_BYTES = 56 * 1024 * 1024


def _dot(a, b):
    return jnp.dot(a.astype(BF16), b.astype(BF16), preferred_element_type=F32)


def _dot_tb(a, b):
    return lax.dot_general(a.astype(BF16), b.astype(BF16), (((1,), (1,)), ((), ())),
                           preferred_element_type=F32)


def _dot_ta(a, b):
    return lax.dot_general(a.astype(BF16), b.astype(BF16), (((0,), (0,)), ((), ())),
                           preferred_element_type=F32)


def _sigmoid(x):
    return jax.nn.sigmoid(x)


def _silu(x):
    return x * _sigmoid(x)


def _log1pexp_neg_abs(x):
    return jnp.log(1.0 + jnp.exp(-jnp.abs(x)))


def _softplus(x):
    return jnp.maximum(x, 0.0) + _log1pexp_neg_abs(x)


def _rms(x, g):
    return x * lax.rsqrt(jnp.mean(x * x, axis=-1, keepdims=True) + NORM_EPS) * g


def _iota(shape, dim):
    return lax.broadcasted_iota(jnp.int32, shape, dim)


def _segment_cumsum(x, seg):
    tile = 8
    sub = _iota(x.shape, 0) % tile
    sh = 1
    while sh < tile:
        x = x + jnp.where(sub >= sh, pltpu.roll(x, sh, axis=0), 0.0)
        sh *= 2
    per = seg // tile
    out, last, carry = [], [], None
    for j in range(x.shape[0] // tile):
        blk = x[j * tile:(j + 1) * tile, :]
        tot = blk[tile - 1:tile, :]
        if j % per:
            blk, tot = blk + carry, tot + carry
        out.append(blk)
        carry = tot
        if j % per == per - 1:
            last += [jnp.broadcast_to(tot, (seg, x.shape[1]))]
    return jnp.concatenate(out, axis=0), jnp.concatenate(last, axis=0)


def _full_spec(shape):
    nd = len(shape)
    return pl.BlockSpec(shape, lambda *_: (0,) * nd)


def _params(**flags):
    return pltpu.CompilerParams(dimension_semantics=("arbitrary", "arbitrary"),
                                vmem_limit_bytes=VMEM_LIMIT_BYTES, flags=flags or None)


def _ada_kernel(c_ref, w_ref, b_ref, o_ref):
    cond = _silu(c_ref[...].astype(F32))
    o_ref[0] = jnp.dot(cond, w_ref[0].astype(F32), preferred_element_type=F32,
                       precision=lax.Precision.HIGHEST) + b_ref[0]


def _ada_all(c, ada_w, ada_b):
    depth, d, d3 = ada_w.shape
    bsz = c.shape[0]
    tn = 1024
    return pl.pallas_call(
        _ada_kernel,
        grid=(depth, d3 // tn),
        in_specs=[pl.BlockSpec((bsz, d), lambda l, n: (0, 0)),
                  pl.BlockSpec((1, d, tn), lambda l, n: (l, 0, n)),
                  pl.BlockSpec((1, 1, tn), lambda l, n: (l, 0, n))],
        out_specs=pl.BlockSpec((1, bsz, tn), lambda l, n: (l, 0, n)),
        out_shape=jax.ShapeDtypeStruct((depth, bsz, d3), F32),
        compiler_params=_params(),
        name="ada_ln",
    )(c, ada_w, ada_b.reshape(depth, 1, d3))


def _lb_kernel(logit_ref, o_ref):
    x = logit_ref[...].astype(F32)
    n = x.shape[0]
    m = jnp.max(x, axis=0, keepdims=True)
    e = jnp.exp(x - m)
    sm = e / jnp.sum(e, axis=0, keepdims=True)
    acc = jnp.zeros_like(sm[0:1])
    for i in range(n):
        if i > 0:
            acc = acc + sm[i:i + 1]
        o_ref[i:i + 1, :] = acc


def _lower_bounds(logits):
    return pl.pallas_call(
        _lb_kernel,
        out_shape=jax.ShapeDtypeStruct(logits.shape, F32),
        name="hgrn_lower_bounds",
    )(logits)


def _even_kernel(res_ref, ada_ref, preg_ref, postg_ref, win_ref, wout_ref, poolw_ref, pscale_ref,
                 lb_ref, og_ref, out_ref,
                 pbuf, st_ref, hq_ref, hk_ref, hc_ref, hv_ref, ho_ref):
    ti = pl.program_id(1)
    tt, d = res_ref.shape[1], res_ref.shape[2]
    pw = pbuf.shape[1]
    nh = pw // HGRN_HEAD
    sub = MAX_WINDOW
    pad = MAX_WINDOW

    @pl.when(ti == 0)
    def _():
        pbuf[0:pad, :] = jnp.zeros((pad, pw), F32)
        st_ref[...] = jnp.zeros(st_ref.shape, F32)

    x = res_ref[0]
    ada = ada_ref[0]
    shift, scale, gate = ada[:, :d], ada[:, d:2 * d], ada[:, 2 * d:]
    h = _rms(x, preg_ref[...]) * (1.0 + scale) + shift
    u = _dot(h, win_ref[...])

    row = _iota((tt, 1), 0)
    up = u[:, :pw]
    pbuf[pad:pad + tt, :] = up
    pos1 = (ti * tt + row + 1).astype(F32)
    gw = pw // len(POOL_WINDOWS)
    pooled = []
    for gi, win in enumerate(POOL_WINDOWS):
        sl = slice(gi * gw, (gi + 1) * gw)
        s = pbuf[:, sl]
        sh = 1
        while sh < win:
            s = s + pltpu.roll(s, sh, axis=0)
            sh *= 2
        p = s[pad:, :] / jnp.minimum(pos1, float(win)) - up[:, sl]
        pooled.append(_dot(p, poolw_ref[gi]))
    pbuf[0:pad, :] = pbuf[tt:tt + pad, :]
    y_pool = jnp.concatenate(pooled, axis=-1) * pscale_ref[...]

    q = _silu(u[:, pw:2 * pw])
    fr = u[:, 2 * pw:3 * pw]
    v = u[:, 3 * pw:4 * pw]
    z = u[:, 4 * pw:]
    lb = lb_ref[...]
    k = (1.0 - lb) * _sigmoid(-fr)
    la = jnp.log(lb)
    lsig = jnp.minimum(fr, 0.0) - _log1pexp_neg_abs(fr)
    lbb = jnp.log1p(-lb) + lsig
    logf = jnp.maximum(la, lbb) + _log1pexp_neg_abs(la - lbb)

    cs, csl = _segment_cumsum(logf, sub)
    qt = q * jnp.exp(cs)
    kvec = k * jnp.exp(csl - cs)
    dl = jnp.exp(csl)
    nsub = tt // sub
    heads = [slice(hh * HGRN_HEAD, (hh + 1) * HGRN_HEAD) for hh in range(nh)]
    subs = [slice(m * sub, (m + 1) * sub) for m in range(nsub)]

    upd = [[_dot_ta(v[rows, hs], kvec[rows, hs]) for hs in heads] for rows in subs]
    start = []
    states = [st_ref[hh] for hh in range(nh)]
    for m, rows in enumerate(subs):
        start.append([s.astype(BF16) for s in states])
        states = [s * dl[m * sub:m * sub + 1, hs] + upd[m][hh]
                  for hh, (s, hs) in enumerate(zip(states, heads))]
    for hh in range(nh):
        st_ref[hh] = states[hh]
    o_inter = [jnp.concatenate([_dot_tb(qt[rows, hs], start[m][hh]) for m, rows in enumerate(subs)],
                               axis=0) for hh, hs in enumerate(heads)]

    for hh, hs in enumerate(heads):
        hq_ref[hh] = q[:, hs]
        hk_ref[hh] = k[:, hs]
        hc_ref[hh] = cs[:, hs] * LOG2_E
        hv_ref[hh] = v[:, hs]
    o_heads = []
    for hh in range(nh):
        at = lambda ref, t: ref[hh, pl.ds(t, nsub, stride=sub), :]
        qs = [at(hq_ref, t) for t in range(sub)]
        ks = [at(hk_ref, t) for t in range(sub)]
        cc = [at(hc_ref, t) for t in range(sub)]
        vv = [at(hv_ref, t) for t in range(sub)]
        for t in range(sub):
            acc = jnp.sum(qs[t] * ks[t], axis=-1, keepdims=True) * vv[t]
            for s in range(t):
                p = qs[t] * ks[s] * jnp.exp2(cc[t] - cc[s])
                acc = acc + jnp.sum(p, axis=-1, keepdims=True) * vv[s]
            ho_ref[hh, pl.ds(t, nsub, stride=sub), :] = acc
        o_heads.append(o_inter[hh] + ho_ref[hh])
    og = og_ref[...]
    y_h = [o * lax.rsqrt(jnp.mean(o * o, axis=-1, keepdims=True) + NORM_EPS)
           * og[:, hh * HGRN_HEAD:(hh + 1) * HGRN_HEAD] for hh, o in enumerate(o_heads)]

    y = jnp.concatenate([y_pool] + y_h, axis=-1) * _silu(z)
    out = _dot(y, wout_ref[...])
    out_ref[0] = x + gate * _rms(out, postg_ref[...])


def _even_layer(res, ada, pre_g, post_g, w_in, w_out, pool_w, pool_scale, lb, onorm_g):
    bsz, t, d = res.shape
    tt = min(ROW_TILE, t)
    pw = pool_scale.shape[-1]
    nh = pw // HGRN_HEAD
    row_spec = pl.BlockSpec((1, tt, d), lambda b, i: (b, i, 0))
    return pl.pallas_call(
        _even_kernel,
        grid=(bsz, t // tt),
        in_specs=[row_spec,
                  pl.BlockSpec((1, 1, 3 * d), lambda b, i: (b, 0, 0)),
                  _full_spec((1, d)), _full_spec((1, d)),
                  _full_spec(w_in.shape), _full_spec(w_out.shape), _full_spec(pool_w.shape),
                  _full_spec((1, pw)), _full_spec((1, pw)), _full_spec((1, pw))],
        out_specs=row_spec,
        out_shape=jax.ShapeDtypeStruct(res.shape, F32),
        scratch_shapes=[pltpu.VMEM((MAX_WINDOW + tt, pw), F32),
                        pltpu.VMEM((nh, HGRN_HEAD, HGRN_HEAD), F32)]
        + [pltpu.VMEM((nh, tt, HGRN_HEAD), F32) for _ in range(5)],
        compiler_params=_params(),
        name="even_layer",
    )(res, ada.reshape(bsz, 1, 3 * d), pre_g.reshape(1, d), post_g.reshape(1, d),
      w_in.astype(BF16), w_out.astype(BF16), pool_w.astype(BF16),
      pool_scale.reshape(1, pw), lb.reshape(1, pw), onorm_g.reshape(1, pw))


def _head_reduce_matrix(d):
    return jnp.where(_iota((d, HEAD_SLOTS), 0) // RWKV_HEAD == _iota((d, HEAD_SLOTS), 1), 1.0, 0.0).astype(BF16)


def _head_spread_matrix(d):
    return jnp.where(_iota((HEAD_SLOTS, d), 0) == _iota((HEAD_SLOTS, d), 1) // RWKV_HEAD, 1.0, 0.0).astype(BF16)


def _odd_pre_kernel(has_vfirst, *refs):
    if has_vfirst:
        (res_ref, ada_ref, preg_ref, mu_ref, w_ref, w0_ref, w1_ref, w2_ref, a0_ref, a1_ref, a2_ref,
         kk_ref, ka_ref, vf_ref, v0_ref, v1_ref, v2_ref,
         r_out, lw_out, k_out, v_out, kk_out, b_out, z_out, prev_ref) = refs
    else:
        (res_ref, ada_ref, preg_ref, mu_ref, w_ref, w0_ref, w1_ref, w2_ref, a0_ref, a1_ref, a2_ref,
         kk_ref, ka_ref,
         r_out, lw_out, k_out, v_out, kk_out, b_out, z_out, prev_ref) = refs
    ti = pl.program_id(1)
    tt, d = res_ref.shape[1], res_ref.shape[2]

    @pl.when(ti == 0)
    def _():
        prev_ref[...] = jnp.zeros(prev_ref.shape, F32)

    x = res_ref[0]
    ada = ada_ref[0]
    shift, scale = ada[:, :d], ada[:, d:2 * d]
    h = _rms(x, preg_ref[...]) * (1.0 + scale) + shift
    row = _iota((tt, 1), 0)
    hs = jnp.where(row == 0, prev_ref[0:1, :], pltpu.roll(h, 1, axis=0))
    prev_ref[0:1, :] = h[tt - 1:tt, :]
    xx = hs - h
    xm = lambda p: h + xx * mu_ref[p:p + 1, :]

    r = _dot(xm(0), w_ref[0])
    k = _dot(xm(1), w_ref[1])
    xv = xm(2)
    v = _dot(xv, w_ref[2])
    z_out[0] = _dot(xm(3), w_ref[3])
    logw = -_softplus(-(w0_ref[...] + _dot(jnp.tanh(_dot(xm(4), w1_ref[...])), w2_ref[...]))) - 0.5
    lw_out[0] = -jnp.exp(logw)
    a = _sigmoid(a0_ref[...] + _dot(_dot(xm(5), a1_ref[...]), a2_ref[...]))
    if has_vfirst:
        v = v + (vf_ref[0] - v) * _sigmoid(v0_ref[...] + _dot(_dot(xv, v1_ref[...]), v2_ref[...]))
    kkr = k * kk_ref[...]
    ss = _dot(kkr * kkr, _head_reduce_matrix(d))
    kk = kkr * _dot(1.0 / jnp.maximum(jnp.sqrt(ss), 1e-12), _head_spread_matrix(d))
    r_out[0] = r
    k_out[0] = k * (1.0 + (a - 1.0) * ka_ref[...])
    v_out[0] = v
    kk_out[0] = kk
    b_out[0] = kk * a


def _odd_pre(res, ada, pre_g, mu, w_rkvz, w0, w1, w2, a0, a1, a2, k_k, k_a, vfirst=None):
    bsz, t, d = res.shape
    tt = min(ROW_TILE, t)
    row_spec = pl.BlockSpec((1, tt, d), lambda b, i: (b, i, 0))
    vec = lambda a: a.reshape(1, d)
    args = [res, ada.reshape(bsz, 1, 3 * d), vec(pre_g), mu, w_rkvz.astype(BF16), vec(w0),
            w1.astype(BF16), w2.astype(BF16), vec(a0), a1.astype(BF16), a2.astype(BF16),
            vec(k_k), vec(k_a)]
    specs = [row_spec, pl.BlockSpec((1, 1, 3 * d), lambda b, i: (b, 0, 0)), _full_spec((1, d)),
             _full_spec(mu.shape), _full_spec(w_rkvz.shape), _full_spec((1, d)),
             _full_spec(w1.shape), _full_spec(w2.shape), _full_spec((1, d)),
             _full_spec(a1.shape), _full_spec(a2.shape), _full_spec((1, d)), _full_spec((1, d))]
    if vfirst is not None:
        v_first, v0, v1, v2 = vfirst
        args += [v_first, vec(v0), v1.astype(BF16), v2.astype(BF16)]
        specs += [row_spec, _full_spec((1, d)), _full_spec(v1.shape), _full_spec(v2.shape)]
    return pl.pallas_call(
        functools.partial(_odd_pre_kernel, vfirst is not None),
        grid=(bsz, t // tt),
        in_specs=specs,
        out_specs=[row_spec] * 7,
        out_shape=[jax.ShapeDtypeStruct(res.shape, F32)] * 7,
        scratch_shapes=[pltpu.VMEM((8, d), F32)],
        compiler_params=_params(),
        name="rwkv_pre",
    )(*args)


def _block_rows(y, n_heads, head):
    yb = y.astype(BF16)
    lane_head = _iota(yb.shape, 1) // head
    return jnp.concatenate([jnp.where(lane_head == hh, yb, jnp.zeros_like(yb)) for hh in range(n_heads)],
                           axis=0)


def _head_transpose(x, n_heads):
    w = x.shape[1] // n_heads
    return jnp.concatenate([x[:, hh * w:(hh + 1) * w] for hh in range(n_heads)], axis=0).T


def _scan_local_steps(ins, store):
    rs, lws, ks, vs, kks, bs = ins
    c_len, g = rs[0].shape
    nh = g // RWKV_HEAD
    hd = RWKV_HEAD
    each = lambda f, *ls: [f(*xs) for xs in zip(*ls)]
    bd = lambda y: _block_rows(y, nh, hd)
    mm = lambda x, y: _dot(x, bd(y))
    stack = lambda x, y: jnp.concatenate([x.astype(BF16), y.astype(BF16)], axis=0)
    wc = nh * c_len

    t_idx = _iota((c_len, wc), 0)
    s_idx = _iota((c_len, wc), 1) % c_len
    strict = s_idx < t_idx
    incl = s_idx <= t_idx
    diag_blk = strict & ((s_idx // RWKV_SUB) == (t_idx // RWKV_SUB))
    eye = jnp.where(s_idx == t_idx, 1.0, 0.0)
    dia = _iota((c_len, hd), 0) == _iota((c_len, hd), 1)

    e = {}
    steps = []

    def step(f):
        steps.append(f)
        return f

    @step
    def _():
        e["cs"] = each(lambda lw: _segment_cumsum(lw, c_len)[0], lws)

    @step
    def _():
        cs = e["cs"]
        cl = [c[c_len - 1:c_len, :] for c in cs]
        e["rt"] = each(lambda r, c: r * jnp.exp(c), rs, cs)
        e["kkt"] = each(lambda kk, c, lw: kk * jnp.exp(c - lw), kks, cs, lws)
        e["kh"] = each(lambda k, c: k * jnp.exp(-c), ks, cs)
        e["bh"] = each(lambda b, c: b * jnp.exp(-c), bs, cs)
        e["pend"] = [jnp.exp(l) for l in cl]
        e["kvec"] = each(lambda kh, p: kh * p, e["kh"], e["pend"])
        e["bvec"] = each(lambda bh, p: bh * p, e["bh"], e["pend"])

    @step
    def _():
        a_all = each(lambda kkt, rt, bh, kh: _dot_tb(stack(kkt, rt), stack(bd(bh), bd(kh))),
                     e["kkt"], e["rt"], e["bh"], e["kh"])
        a_kb = [a[:c_len, :wc] for a in a_all]
        e["a_kk"] = [jnp.where(strict, a[:c_len, wc:], 0.0) for a in a_all]
        e["a_rb"] = [jnp.where(incl, a[c_len:, :wc], 0.0) for a in a_all]
        e["a_rk"] = [jnp.where(incl, a[c_len:, wc:], 0.0) for a in a_all]
        e["n_p"] = each(lambda a: jnp.where(diag_blk, a, 0.0), a_kb)
        e["n_o"] = each(lambda a: jnp.where(strict & ~diag_blk, a, 0.0), a_kb)
        e["t_d"] = each(lambda n: eye - n, e["n_p"])

    @step
    def _():
        e["n_p"] = each(mm, e["n_p"], e["n_p"])

    for _ in range(RWKV_SUB.bit_length() - 3):
        @step
        def _():
            both = each(lambda t, n: _dot(stack(t, n), bd(n)), e["t_d"], e["n_p"])
            e["t_d"] = each(lambda t, s: t + s[:c_len], e["t_d"], both)
            e["n_p"] = [s[c_len:] for s in both]

    @step
    def _():
        e["t_d"] = each(lambda t, n: t + mm(t, n), e["t_d"], e["n_p"])

    @step
    def _():
        e["z_p"] = each(mm, e["t_d"], e["n_o"])
        e["t_m"] = each(lambda z: eye - z, e["z_p"])

    for _ in range((c_len // RWKV_SUB).bit_length() - 2):
        @step
        def _():
            e["z_p"] = each(mm, e["z_p"], e["z_p"])

        @step
        def _():
            e["t_m"] = each(lambda t, z: t + mm(t, z), e["t_m"], e["z_p"])

    @step
    def _():
        e["tmat"] = each(mm, e["t_m"], e["t_d"])

    @step
    def _():
        e["av"] = each(lambda akk, ark, kv, v: _dot(
            jnp.concatenate([akk.astype(BF16), ark.astype(BF16), _head_transpose(kv, nh).astype(BF16)], axis=0),
            bd(v)), e["a_kk"], e["a_rk"], e["kvec"], vs)

    @step
    def _():
        for i in range(len(rs)):
            pcat = jnp.concatenate(
                [jnp.broadcast_to(jnp.sum(jnp.where(dia, e["pend"][i][:, hh * hd:(hh + 1) * hd], 0.0),
                                          axis=1, keepdims=True), (c_len, hd)) for hh in range(nh)], axis=1)
            av = e["av"][i]
            store(i, kr=stack(e["kkt"][i], e["rt"][i]), tmat=e["tmat"][i].astype(BF16),
                  av=av[:c_len], arkv=av[c_len:2 * c_len], kv=av[2 * c_len:],
                  arbt=stack(e["a_rb"][i], _head_transpose(e["bvec"][i], nh)), pcat=pcat)

    return steps


def _scan_chain_steps(load, n_inst, states, emit):
    g = states[0].shape[1]
    nh = g // RWKV_HEAD
    c_len = RWKV_CHUNK
    bd = lambda y: _block_rows(y, nh, RWKV_HEAD)
    e = {}

    def s1():
        e["x"] = [_dot(load(i, "kr"), bd(states[i])) for i in range(n_inst)]

    def s2():
        e["u"] = [_dot(load(i, "tmat"), bd(e["x"][i][:c_len] + load(i, "av"))) for i in range(n_inst)]

    def s3():
        prod = [_dot(load(i, "arbt"), bd(e["u"][i])) for i in range(n_inst)]
        emit([e["x"][i][c_len:] + load(i, "arkv") - prod[i][:c_len] for i in range(n_inst)])
        for i in range(n_inst):
            states[i] = states[i] * load(i, "pcat") + load(i, "kv") - prod[i][c_len:]

    return [s1, s2, s3]


_SCAN_FIELDS = ("kr", "tmat", "av", "arkv", "arbt", "kv", "pcat")


def _scan_kernel(r_ref, lw_ref, k_ref, v_ref, kk_ref, b_ref, y_ref, mt_ref, *bufs):
    ti = pl.program_id(1)
    tt, d = r_ref.shape[1], r_ref.shape[2]
    ng = d // GROUP_LANES
    nchunk = tt // RWKV_CHUNK
    ninst = nchunk * ng
    in_refs = (r_ref, lw_ref, k_ref, v_ref, kk_ref, b_ref)
    buf = dict(zip(_SCAN_FIELDS, bufs))
    wslot = (ti % 2) * ninst
    rslot = ((ti + 1) % 2) * ninst

    @pl.when(ti == 0)
    def _():
        mt_ref[...] = jnp.zeros(mt_ref.shape, F32)
        for name in _SCAN_FIELDS:
            ref = buf[name]
            ref[pl.ds(ninst, ninst)] = jnp.zeros((ninst,) + ref.shape[1:], ref.dtype)

    def store(i, **fields):
        for name, val in fields.items():
            buf[name][wslot + i] = val

    states = [mt_ref[gi] for gi in range(ng)]
    ys = []
    blocks, chain = [], []
    per_block = SCAN_CHUNKS_PER_STEP * ng
    for b0 in range(0, ninst, per_block):
        idx = range(b0, b0 + per_block)
        ins = [[ref[0, (i // ng) * RWKV_CHUNK:(i // ng + 1) * RWKV_CHUNK,
                    (i % ng) * GROUP_LANES:(i % ng + 1) * GROUP_LANES] for i in idx]
               for ref in in_refs]
        blocks.append(_scan_local_steps(ins, lambda i, _b0=b0, **f: store(_b0 + i, **f)))
    local = []
    for bi, steps in enumerate(blocks):
        at = min(len(local), bi * SCAN_BLOCK_STAGGER)
        merged = local[:at]
        rest = local[at:]
        for k in range(max(len(rest), len(steps))):
            merged += rest[k:k + 1] + steps[k:k + 1]
        local = merged
    for ci in range(nchunk):
        chain += _scan_chain_steps(lambda i, name, _c=ci: buf[name][rslot + _c * ng + i], ng,
                                   states, ys.append)
    done = 0
    span = max(1, (3 * len(local)) // 4)
    for j, f in enumerate(local):
        f()
        want = min(len(chain), -((j + 1) * len(chain) // -span))
        while done < want:
            chain[done]()
            done += 1
    for ci in range(nchunk):
        y_ref[0, ci * RWKV_CHUNK:(ci + 1) * RWKV_CHUNK, :] = jnp.concatenate(ys[ci], axis=-1)
    for gi in range(ng):
        mt_ref[gi] = states[gi]


def _rwkv_scan(r, lw, k, v, kk, b):
    bsz, t, d = r.shape
    tt = min(ROW_TILE, t)
    nt = t // tt
    ng = d // GROUP_LANES
    slots = 2 * (tt // RWKV_CHUNK) * ng
    c, g = RWKV_CHUNK, GROUP_LANES
    in_spec = pl.BlockSpec((1, tt, d), lambda bi, i: (bi, jnp.minimum(i, nt - 1), 0))
    out_spec = pl.BlockSpec((1, tt, d), lambda bi, i: (bi, jnp.maximum(i - 1, 0), 0))
    shapes = dict(kr=((2 * c, g), BF16), tmat=((c, g), BF16), av=((c, g), F32), arkv=((c, g), F32),
                  arbt=((c + RWKV_HEAD, g), BF16), kv=((RWKV_HEAD, g), F32), pcat=((RWKV_HEAD, g), F32))
    return pl.pallas_call(
        _scan_kernel,
        grid=(bsz, nt + 1),
        in_specs=[in_spec] * 6,
        out_specs=out_spec,
        out_shape=jax.ShapeDtypeStruct(r.shape, F32),
        scratch_shapes=[pltpu.VMEM((ng, RWKV_HEAD, g), F32)]
        + [pltpu.VMEM((slots,) + shapes[n][0], shapes[n][1]) for n in _SCAN_FIELDS],
        compiler_params=_params(),
        name="rwkv_scan",
    )(r, lw, k, v, kk, b)


def _odd_post_kernel(res_ref, ada_ref, y_ref, r_ref, k_ref, v_ref, z_ref, rk_ref, lg_ref, lb_ref,
                     postg_ref, wout_ref, out_ref):
    d = res_ref.shape[2]
    tt = res_ref.shape[1]
    red, spread = _head_reduce_matrix(d), _head_spread_matrix(d)
    inv = 1.0 / RWKV_HEAD
    y = y_ref[0]
    rkr = r_ref[0] * k_ref[0] * rk_ref[...]
    sums = _dot(jnp.concatenate([y, rkr], axis=0), red)
    both = _dot(jnp.concatenate([sums[:tt] * inv, sums[tt:]], axis=0), spread)
    yc = y - both[:tt]
    rstd = lax.rsqrt(_dot(yc * yc, red) * inv + LNX_EPS)
    yn = yc * _dot(rstd, spread) * lg_ref[...] + lb_ref[...]
    out = _dot((yn + both[tt:] * v_ref[0]) * _silu(z_ref[0]), wout_ref[...])
    gate = ada_ref[0][:, 2 * d:]
    out_ref[0] = res_ref[0] + gate * _rms(out, postg_ref[...])


def _odd_post(res, ada, y, r, k, v, z, r_k, lnx_g, lnx_b, post_g, w_out):
    bsz, t, d = res.shape
    tt = min(WIDE_ROW_TILE, t)
    row_spec = pl.BlockSpec((1, tt, d), lambda b, i: (b, i, 0))
    vec = lambda a: a.reshape(1, d)
    return pl.pallas_call(
        _odd_post_kernel,
        grid=(bsz, t // tt),
        in_specs=[row_spec, pl.BlockSpec((1, 1, 3 * d), lambda b, i: (b, 0, 0))] + [row_spec] * 5
        + [_full_spec((1, d))] * 4 + [_full_spec(w_out.shape)],
        out_specs=row_spec,
        out_shape=jax.ShapeDtypeStruct(res.shape, F32),
        compiler_params=_params(),
        name="rwkv_post",
    )(res, ada.reshape(bsz, 1, 3 * d), y, r, k, v, z, vec(r_k), vec(lnx_g), vec(lnx_b), vec(post_g),
      w_out.astype(BF16))


def kernel(x, c, ada_w, ada_b, pre_g, post_g, ev_w_in, ev_w_out, pool_w, pool_scale, hgrn_lb_logits,
           hgrn_onorm_g, rw_mu, rw_w_rkvz, rw_w0, rw_w1, rw_w2, rw_a0, rw_a1, rw_a2, rw_k_k, rw_k_a,
           rw_r_k, rw_lnx_g, rw_lnx_b, rw_w_out, rw_v0, rw_v1, rw_v2):
    depth = ada_w.shape[0]
    res = x.astype(F32)
    ada = _ada_all(c, ada_w, ada_b)
    lb_all = _lower_bounds(hgrn_lb_logits)
    v_first = None
    for layer in range(depth):
        j = layer // 2
        if layer % 2 == 0:
            res = _even_layer(res, ada[layer], pre_g[layer], post_g[layer], ev_w_in[j], ev_w_out[j],
                              pool_w[j], pool_scale[j], lb_all[j], hgrn_onorm_g[j])
        else:
            vfirst = None if v_first is None else (v_first, rw_v0[j - 1], rw_v1[j - 1], rw_v2[j - 1])
            r, lw, k, v, kk, b, z = _odd_pre(res, ada[layer], pre_g[layer], rw_mu[j], rw_w_rkvz[j],
                                             rw_w0[j], rw_w1[j], rw_w2[j], rw_a0[j], rw_a1[j],
                                             rw_a2[j], rw_k_k[j], rw_k_a[j], vfirst)
            if v_first is None:
                v_first = v
            y = _rwkv_scan(r, lw, k, v, kk, b)
            res = _odd_post(res, ada[layer], y, r, k, v, z, rw_r_k[j], rw_lnx_g[j], rw_lnx_b[j],
                            post_g[layer], rw_w_out[j])
    return res.astype(x.dtype)
```

```python
import functools

import jax
import jax.numpy as jnp
from jax import lax
from jax.experimental import pallas as pl
from jax.experimental.pallas import tpu as pltpu

F32 = jnp.float32
BF16 = jnp.bfloat16

NORM_EPS = 1e-6
LOG2_E = 1.4426950408889634
LNX_EPS = 64e-5
POOL_WINDOWS = (2, 4, 8, 16)
MAX_WINDOW = 16
HGRN_HEAD = 128
RWKV_HEAD = 64
RWKV_CHUNK = 64
RWKV_SUB = 16
HEAD_SLOTS = 128
GROUP_LANES = 256
SCAN_CHUNKS_PER_STEP = 2
SCAN_BLOCK_STAGGER = 5
ROW_TILE = 256
WIDE_ROW_TILE = 512
VMEM_LIMIT_BYTES = 60000 * 1024


def _dot(a, b):
    return jnp.dot(a.astype(BF16), b.astype(BF16), preferred_element_type=F32)


def _dot_tb(a, b):
    return lax.dot_general(a.astype(BF16), b.astype(BF16), (((1,), (1,)), ((), ())),
                           preferred_element_type=F32)


def _dot_ta(a, b):
    return lax.dot_general(a.astype(BF16), b.astype(BF16), (((0,), (0,)), ((), ())),
                           preferred_element_type=F32)


def _sigmoid(x):
    return jax.nn.sigmoid(x)


def _silu(x):
    return x * _sigmoid(x)


def _log1pexp_neg_abs(x):
    return jnp.log(1.0 + jnp.exp(-jnp.abs(x)))


def _softplus(x):
    return jnp.maximum(x, 0.0) + _log1pexp_neg_abs(x)


def _rms(x, g):
    return x * lax.rsqrt(jnp.mean(x * x, axis=-1, keepdims=True) + NORM_EPS) * g


def _iota(shape, dim):
    return lax.broadcasted_iota(jnp.int32, shape, dim)


def _segment_cumsum(x, seg):
    tile = 8
    sub = _iota(x.shape, 0) % tile
    sh = 1
    while sh < tile:
        x = x + jnp.where(sub >= sh, pltpu.roll(x, sh, axis=0), 0.0)
        sh *= 2
    per = seg // tile
    out, last, carry = [], [], None
    for j in range(x.shape[0] // tile):
        blk = x[j * tile:(j + 1) * tile, :]
        tot = blk[tile - 1:tile, :]
        if j % per:
            blk, tot = blk + carry, tot + carry
        out.append(blk)
        carry = tot
        if j % per == per - 1:
            last += [jnp.broadcast_to(tot, (seg, x.shape[1]))]
    return jnp.concatenate(out, axis=0), jnp.concatenate(last, axis=0)


def _full_spec(shape, single_buffer=False):
    nd = len(shape)
    mode = pl.Buffered(1) if single_buffer else None
    return pl.BlockSpec(shape, lambda *_: (0,) * nd, pipeline_mode=mode)


def _params(**flags):
    return pltpu.CompilerParams(dimension_semantics=("arbitrary", "arbitrary"),
                                vmem_limit_bytes=VMEM_LIMIT_BYTES, flags=flags or None)


def _ada_kernel(c_ref, w_ref, b_ref, o_ref):
    cond = _silu(c_ref[...].astype(F32))
    o_ref[0] = jnp.dot(cond, w_ref[0].astype(F32), preferred_element_type=F32,
                       precision=lax.Precision.HIGHEST) + b_ref[0]


def _ada_all(c, ada_w, ada_b):
    depth, d, d3 = ada_w.shape
    bsz = c.shape[0]
    tn = 1024
    return pl.pallas_call(
        _ada_kernel,
        grid=(depth, d3 // tn),
        in_specs=[pl.BlockSpec((bsz, d), lambda l, n: (0, 0)),
                  pl.BlockSpec((1, d, tn), lambda l, n: (l, 0, n)),
                  pl.BlockSpec((1, 1, tn), lambda l, n: (l, 0, n))],
        out_specs=pl.BlockSpec((1, bsz, tn), lambda l, n: (l, 0, n)),
        out_shape=jax.ShapeDtypeStruct((depth, bsz, d3), F32),
        compiler_params=_params(),
        name="ada_ln",
    )(c, ada_w, ada_b.reshape(depth, 1, d3))


def _lb_kernel(logit_ref, o_ref):
    x = logit_ref[...].astype(F32)
    n = x.shape[0]
    m = jnp.max(x, axis=0, keepdims=True)
    e = jnp.exp(x - m)
    sm = e / jnp.sum(e, axis=0, keepdims=True)
    acc = jnp.zeros_like(sm[0:1])
    for i in range(n):
        if i > 0:
            acc = acc + sm[i:i + 1]
        o_ref[i:i + 1, :] = acc


def _lower_bounds(logits):
    return pl.pallas_call(
        _lb_kernel,
        out_shape=jax.ShapeDtypeStruct(logits.shape, F32),
        name="hgrn_lower_bounds",
    )(logits)


def _even_kernel(res_ref, ada_ref, preg_ref, postg_ref, win_ref, wout_ref, poolw_ref, pscale_ref,
                 lb_ref, og_ref, out_ref,
                 pbuf, st_ref, hq_ref, hk_ref, hc_ref, hv_ref, ho_ref):
    ti = pl.program_id(1)
    tt, d = res_ref.shape[1], res_ref.shape[2]
    pw = pbuf.shape[1]
    nh = pw // HGRN_HEAD
    sub = MAX_WINDOW
    pad = MAX_WINDOW

    @pl.when(ti == 0)
    def _():
        pbuf[0:pad, :] = jnp.zeros((pad, pw), F32)
        st_ref[...] = jnp.zeros(st_ref.shape, F32)

    x = res_ref[0]
    ada = ada_ref[0]
    shift, scale, gate = ada[:, :d], ada[:, d:2 * d], ada[:, 2 * d:]
    h = _rms(x, preg_ref[...]) * (1.0 + scale) + shift
    u = _dot(h, win_ref[...])

    row = _iota((tt, 1), 0)
    up = u[:, :pw]
    pbuf[pad:pad + tt, :] = up
    pos1 = (ti * tt + row + 1).astype(F32)
    gw = pw // len(POOL_WINDOWS)
    pooled = []
    for gi, win in enumerate(POOL_WINDOWS):
        sl = slice(gi * gw, (gi + 1) * gw)
        s = pbuf[:, sl]
        sh = 1
        while sh < win:
            s = s + pltpu.roll(s, sh, axis=0)
            sh *= 2
        p = s[pad:, :] / jnp.minimum(pos1, float(win)) - up[:, sl]
        pooled.append(_dot(p, poolw_ref[gi]))
    pbuf[0:pad, :] = pbuf[tt:tt + pad, :]
    y_pool = jnp.concatenate(pooled, axis=-1) * pscale_ref[...]

    q = _silu(u[:, pw:2 * pw])
    fr = u[:, 2 * pw:3 * pw]
    v = u[:, 3 * pw:4 * pw]
    z = u[:, 4 * pw:]
    lb = lb_ref[...]
    k = (1.0 - lb) * _sigmoid(-fr)
    la = jnp.log(lb)
    lsig = jnp.minimum(fr, 0.0) - _log1pexp_neg_abs(fr)
    lbb = jnp.log1p(-lb) + lsig
    logf = jnp.maximum(la, lbb) + _log1pexp_neg_abs(la - lbb)

    cs, csl = _segment_cumsum(logf, sub)
    qt = q * jnp.exp(cs)
    kvec = k * jnp.exp(csl - cs)
    dl = jnp.exp(csl)
    nsub = tt // sub
    heads = [slice(hh * HGRN_HEAD, (hh + 1) * HGRN_HEAD) for hh in range(nh)]
    subs = [slice(m * sub, (m + 1) * sub) for m in range(nsub)]

    upd = [[_dot_ta(v[rows, hs], kvec[rows, hs]) for hs in heads] for rows in subs]
    start = []
    states = [st_ref[hh] for hh in range(nh)]
    for m, rows in enumerate(subs):
        start.append([s.astype(BF16) for s in states])
        states = [s * dl[m * sub:m * sub + 1, hs] + upd[m][hh]
                  for hh, (s, hs) in enumerate(zip(states, heads))]
    for hh in range(nh):
        st_ref[hh] = states[hh]
    o_inter = [jnp.concatenate([_dot_tb(qt[rows, hs], start[m][hh]) for m, rows in enumerate(subs)],
                               axis=0) for hh, hs in enumerate(heads)]

    for hh, hs in enumerate(heads):
        hq_ref[hh] = q[:, hs]
        hk_ref[hh] = k[:, hs]
        hc_ref[hh] = cs[:, hs] * LOG2_E
        hv_ref[hh] = v[:, hs]
    o_heads = []
    for hh in range(nh):
        at = lambda ref, t: ref[hh, pl.ds(t, nsub, stride=sub), :]
        qs = [at(hq_ref, t) for t in range(sub)]
        ks = [at(hk_ref, t) for t in range(sub)]
        cc = [at(hc_ref, t) for t in range(sub)]
        vv = [at(hv_ref, t) for t in range(sub)]
        for t in range(sub):
            acc = jnp.sum(qs[t] * ks[t], axis=-1, keepdims=True) * vv[t]
            for s in range(t):
                p = qs[t] * ks[s] * jnp.exp2(cc[t] - cc[s])
                acc = acc + jnp.sum(p, axis=-1, keepdims=True) * vv[s]
            ho_ref[hh, pl.ds(t, nsub, stride=sub), :] = acc
        o_heads.append(o_inter[hh] + ho_ref[hh])
    og = og_ref[...]
    y_h = [o * lax.rsqrt(jnp.mean(o * o, axis=-1, keepdims=True) + NORM_EPS)
           * og[:, hh * HGRN_HEAD:(hh + 1) * HGRN_HEAD] for hh, o in enumerate(o_heads)]

    y = jnp.concatenate([y_pool] + y_h, axis=-1) * _silu(z)
    out = _dot(y, wout_ref[...])
    out_ref[0] = x + gate * _rms(out, postg_ref[...])


def _even_layer(res, ada, pre_g, post_g, w_in, w_out, pool_w, pool_scale, lb, onorm_g):
    bsz, t, d = res.shape
    tt = min(WIDE_ROW_TILE, t)
    pw = pool_scale.shape[-1]
    nh = pw // HGRN_HEAD
    row_spec = pl.BlockSpec((1, tt, d), lambda b, i: (b, i, 0))
    return pl.pallas_call(
        _even_kernel,
        grid=(bsz, t // tt),
        in_specs=[row_spec,
                  pl.BlockSpec((1, 1, 3 * d), lambda b, i: (b, 0, 0)),
                  _full_spec((1, d)), _full_spec((1, d)),
                  _full_spec(w_in.shape), _full_spec(w_out.shape), _full_spec(pool_w.shape),
                  _full_spec((1, pw)), _full_spec((1, pw)), _full_spec((1, pw))],
        out_specs=row_spec,
        out_shape=jax.ShapeDtypeStruct(res.shape, F32),
        scratch_shapes=[pltpu.VMEM((MAX_WINDOW + tt, pw), F32),
                        pltpu.VMEM((nh, HGRN_HEAD, HGRN_HEAD), F32)]
        + [pltpu.VMEM((nh, tt, HGRN_HEAD), F32) for _ in range(5)],
        compiler_params=_params(),
        name="even_layer",
    )(res, ada.reshape(bsz, 1, 3 * d), pre_g.reshape(1, d), post_g.reshape(1, d),
      w_in.astype(BF16), w_out.astype(BF16), pool_w.astype(BF16),
      pool_scale.reshape(1, pw), lb.reshape(1, pw), onorm_g.reshape(1, pw))


def _head_reduce_matrix(d):
    return jnp.where(_iota((d, HEAD_SLOTS), 0) // RWKV_HEAD == _iota((d, HEAD_SLOTS), 1), 1.0, 0.0).astype(BF16)


def _head_spread_matrix(d):
    return jnp.where(_iota((HEAD_SLOTS, d), 0) == _iota((HEAD_SLOTS, d), 1) // RWKV_HEAD, 1.0, 0.0).astype(BF16)


def _odd_pre_kernel(has_vfirst, *refs):
    if has_vfirst:
        (res_ref, ada_ref, preg_ref, mu_ref, w_ref, w0_ref, w1_ref, w2_ref, a0_ref, a1_ref, a2_ref,
         kk_ref, ka_ref, vf_ref, v0_ref, v1_ref, v2_ref,
         r_out, lw_out, k_out, v_out, kk_out, b_out, z_out, prev_ref) = refs
    else:
        (res_ref, ada_ref, preg_ref, mu_ref, w_ref, w0_ref, w1_ref, w2_ref, a0_ref, a1_ref, a2_ref,
         kk_ref, ka_ref,
         r_out, lw_out, k_out, v_out, kk_out, b_out, z_out, prev_ref) = refs
    ti = pl.program_id(1)
    tt, d = res_ref.shape[1], res_ref.shape[2]

    @pl.when(ti == 0)
    def _():
        prev_ref[...] = jnp.zeros(prev_ref.shape, F32)

    x = res_ref[0]
    ada = ada_ref[0]
    shift, scale = ada[:, :d], ada[:, d:2 * d]
    h = _rms(x, preg_ref[...]) * (1.0 + scale) + shift
    row = _iota((tt, 1), 0)
    hs = jnp.where(row == 0, prev_ref[0:1, :], pltpu.roll(h, 1, axis=0))
    prev_ref[0:1, :] = h[tt - 1:tt, :]
    xx = hs - h
    xm = lambda p: h + xx * mu_ref[p:p + 1, :]

    r = _dot(xm(0), w_ref[0])
    k = _dot(xm(1), w_ref[1])
    xv = xm(2)
    v = _dot(xv, w_ref[2])
    z_out[0] = _dot(xm(3), w_ref[3])
    logw = -_softplus(-(w0_ref[...] + _dot(jnp.tanh(_dot(xm(4), w1_ref[...])), w2_ref[...]))) - 0.5
    lw_out[0] = -jnp.exp(logw)
    a = _sigmoid(a0_ref[...] + _dot(_dot(xm(5), a1_ref[...]), a2_ref[...]))
    if has_vfirst:
        v = v + (vf_ref[0] - v) * _sigmoid(v0_ref[...] + _dot(_dot(xv, v1_ref[...]), v2_ref[...]))
    kkr = k * kk_ref[...]
    ss = _dot(kkr * kkr, _head_reduce_matrix(d))
    kk = kkr * _dot(1.0 / jnp.maximum(jnp.sqrt(ss), 1e-12), _head_spread_matrix(d))
    r_out[0] = r
    k_out[0] = k * (1.0 + (a - 1.0) * ka_ref[...])
    v_out[0] = v
    kk_out[0] = kk
    b_out[0] = kk * a


def _odd_pre(res, ada, pre_g, mu, w_rkvz, w0, w1, w2, a0, a1, a2, k_k, k_a, vfirst=None):
    bsz, t, d = res.shape
    tt = min(WIDE_ROW_TILE, t)
    row_spec = pl.BlockSpec((1, tt, d), lambda b, i: (b, i, 0))
    vec = lambda a: a.reshape(1, d)
    args = [res, ada.reshape(bsz, 1, 3 * d), vec(pre_g), mu, w_rkvz.astype(BF16), vec(w0),
            w1.astype(BF16), w2.astype(BF16), vec(a0), a1.astype(BF16), a2.astype(BF16),
            vec(k_k), vec(k_a)]
    specs = [row_spec, pl.BlockSpec((1, 1, 3 * d), lambda b, i: (b, 0, 0)), _full_spec((1, d)),
             _full_spec(mu.shape), _full_spec(w_rkvz.shape, single_buffer=True), _full_spec((1, d)),
             _full_spec(w1.shape), _full_spec(w2.shape), _full_spec((1, d)),
             _full_spec(a1.shape), _full_spec(a2.shape), _full_spec((1, d)), _full_spec((1, d))]
    if vfirst is not None:
        v_first, v0, v1, v2 = vfirst
        args += [v_first, vec(v0), v1.astype(BF16), v2.astype(BF16)]
        specs += [row_spec, _full_spec((1, d)), _full_spec(v1.shape), _full_spec(v2.shape)]
    return pl.pallas_call(
        functools.partial(_odd_pre_kernel, vfirst is not None),
        grid=(bsz, t // tt),
        in_specs=specs,
        out_specs=[row_spec] * 7,
        out_shape=[jax.ShapeDtypeStruct(res.shape, F32)] * 7,
        scratch_shapes=[pltpu.VMEM((8, d), F32)],
        compiler_params=_params(),
        name="rwkv_pre",
    )(*args)


def _block_rows(y, n_heads, head):
    yb = y.astype(BF16)
    lane_head = _iota(yb.shape, 1) // head
    return jnp.concatenate([jnp.where(lane_head == hh, yb, jnp.zeros_like(yb)) for hh in range(n_heads)],
                           axis=0)


def _head_transpose(x, n_heads):
    w = x.shape[1] // n_heads
    return jnp.concatenate([x[:, hh * w:(hh + 1) * w] for hh in range(n_heads)], axis=0).T


def _scan_local_steps(ins, store):
    rs, lws, ks, vs, kks, bs = ins
    c_len, g = rs[0].shape
    nh = g // RWKV_HEAD
    hd = RWKV_HEAD
    each = lambda f, *ls: [f(*xs) for xs in zip(*ls)]
    bd = lambda y: _block_rows(y, nh, hd)
    mm = lambda x, y: _dot(x, bd(y))
    stack = lambda x, y: jnp.concatenate([x.astype(BF16), y.astype(BF16)], axis=0)
    wc = nh * c_len

    t_idx = _iota((c_len, wc), 0)
    s_idx = _iota((c_len, wc), 1) % c_len
    strict = s_idx < t_idx
    incl = s_idx <= t_idx
    diag_blk = strict & ((s_idx // RWKV_SUB) == (t_idx // RWKV_SUB))
    eye = jnp.where(s_idx == t_idx, 1.0, 0.0)
    dia = _iota((c_len, hd), 0) == _iota((c_len, hd), 1)

    e = {}
    steps = []

    def step(f):
        steps.append(f)
        return f

    @step
    def _():
        e["cs"] = each(lambda lw: _segment_cumsum(lw, c_len)[0], lws)

    @step
    def _():
        cs = e["cs"]
        cl = [c[c_len - 1:c_len, :] for c in cs]
        e["rt"] = each(lambda r, c: r * jnp.exp(c), rs, cs)
        e["kkt"] = each(lambda kk, c, lw: kk * jnp.exp(c - lw), kks, cs, lws)
        e["kh"] = each(lambda k, c: k * jnp.exp(-c), ks, cs)
        e["bh"] = each(lambda b, c: b * jnp.exp(-c), bs, cs)
        e["pend"] = [jnp.exp(l) for l in cl]
        e["kvec"] = each(lambda kh, p: kh * p, e["kh"], e["pend"])
        e["bvec"] = each(lambda bh, p: bh * p, e["bh"], e["pend"])

    @step
    def _():
        a_all = each(lambda kkt, rt, bh, kh: _dot_tb(stack(kkt, rt), stack(bd(bh), bd(kh))),
                     e["kkt"], e["rt"], e["bh"], e["kh"])
        a_kb = [a[:c_len, :wc] for a in a_all]
        e["a_kk"] = [jnp.where(strict, a[:c_len, wc:], 0.0) for a in a_all]
        e["a_rb"] = [jnp.where(incl, a[c_len:, :wc], 0.0) for a in a_all]
        e["a_rk"] = [jnp.where(incl, a[c_len:, wc:], 0.0) for a in a_all]
        e["n_p"] = each(lambda a: jnp.where(diag_blk, a, 0.0), a_kb)
        e["n_o"] = each(lambda a: jnp.where(strict & ~diag_blk, a, 0.0), a_kb)
        e["t_d"] = each(lambda n: eye - n, e["n_p"])

    @step
    def _():
        e["n_p"] = each(mm, e["n_p"], e["n_p"])

    for _ in range(RWKV_SUB.bit_length() - 3):
        @step
        def _():
            both = each(lambda t, n: _dot(stack(t, n), bd(n)), e["t_d"], e["n_p"])
            e["t_d"] = each(lambda t, s: t + s[:c_len], e["t_d"], both)
            e["n_p"] = [s[c_len:] for s in both]

    @step
    def _():
        e["t_d"] = each(lambda t, n: t + mm(t, n), e["t_d"], e["n_p"])

    @step
    def _():
        e["z_p"] = each(mm, e["t_d"], e["n_o"])
        e["t_m"] = each(lambda z: eye - z, e["z_p"])

    for _ in range((c_len // RWKV_SUB).bit_length() - 2):
        @step
        def _():
            e["z_p"] = each(mm, e["z_p"], e["z_p"])

        @step
        def _():
            e["t_m"] = each(lambda t, z: t + mm(t, z), e["t_m"], e["z_p"])

    @step
    def _():
        e["tmat"] = each(mm, e["t_m"], e["t_d"])

    @step
    def _():
        e["av"] = each(lambda akk, ark, kv, v: _dot(
            jnp.concatenate([akk.astype(BF16), ark.astype(BF16), _head_transpose(kv, nh).astype(BF16)], axis=0),
            bd(v)), e["a_kk"], e["a_rk"], e["kvec"], vs)

    @step
    def _():
        for i in range(len(rs)):
            pcat = jnp.concatenate(
                [jnp.broadcast_to(jnp.sum(jnp.where(dia, e["pend"][i][:, hh * hd:(hh + 1) * hd], 0.0),
                                          axis=1, keepdims=True), (c_len, hd)) for hh in range(nh)], axis=1)
            av = e["av"][i]
            store(i, kr=stack(e["kkt"][i], e["rt"][i]), tmat=e["tmat"][i].astype(BF16),
                  av=av[:c_len], arkv=av[c_len:2 * c_len], kv=av[2 * c_len:],
                  arbt=stack(e["a_rb"][i], _head_transpose(e["bvec"][i], nh)), pcat=pcat)

    return steps


def _scan_chain_steps(load, n_inst, states, emit):
    g = states[0].shape[1]
    nh = g // RWKV_HEAD
    c_len = RWKV_CHUNK
    bd = lambda y: _block_rows(y, nh, RWKV_HEAD)
    e = {}

    def s1():
        e["x"] = [_dot(load(i, "kr"), bd(states[i])) for i in range(n_inst)]

    def s2():
        e["u"] = [_dot(load(i, "tmat"), bd(e["x"][i][:c_len] + load(i, "av"))) for i in range(n_inst)]

    def s3():
        prod = [_dot(load(i, "arbt"), bd(e["u"][i])) for i in range(n_inst)]
        emit([e["x"][i][c_len:] + load(i, "arkv") - prod[i][:c_len] for i in range(n_inst)])
        for i in range(n_inst):
            states[i] = states[i] * load(i, "pcat") + load(i, "kv") - prod[i][c_len:]

    return [s1, s2, s3]


_SCAN_FIELDS = ("kr", "tmat", "av", "arkv", "arbt", "kv", "pcat")


def _scan_kernel(r_ref, lw_ref, k_ref, v_ref, kk_ref, b_ref, y_ref, mt_ref, *bufs):
    ti = pl.program_id(1)
    tt, d = r_ref.shape[1], r_ref.shape[2]
    ng = d // GROUP_LANES
    nchunk = tt // RWKV_CHUNK
    ninst = nchunk * ng
    in_refs = (r_ref, lw_ref, k_ref, v_ref, kk_ref, b_ref)
    buf = dict(zip(_SCAN_FIELDS, bufs))
    wslot = (ti % 2) * ninst
    rslot = ((ti + 1) % 2) * ninst

    @pl.when(ti == 0)
    def _():
        mt_ref[...] = jnp.zeros(mt_ref.shape, F32)
        for name in _SCAN_FIELDS:
            ref = buf[name]
            ref[pl.ds(ninst, ninst)] = jnp.zeros((ninst,) + ref.shape[1:], ref.dtype)

    def store(i, **fields):
        for name, val in fields.items():
            buf[name][wslot + i] = val

    states = [mt_ref[gi] for gi in range(ng)]
    ys = []
    blocks, chain = [], []
    per_block = SCAN_CHUNKS_PER_STEP * ng
    for b0 in range(0, ninst, per_block):
        idx = range(b0, b0 + per_block)
        ins = [[ref[0, (i // ng) * RWKV_CHUNK:(i // ng + 1) * RWKV_CHUNK,
                    (i % ng) * GROUP_LANES:(i % ng + 1) * GROUP_LANES] for i in idx]
               for ref in in_refs]
        blocks.append(_scan_local_steps(ins, lambda i, _b0=b0, **f: store(_b0 + i, **f)))
    local = []
    for bi, steps in enumerate(blocks):
        at = min(len(local), bi * SCAN_BLOCK_STAGGER)
        merged = local[:at]
        rest = local[at:]
        for k in range(max(len(rest), len(steps))):
            merged += rest[k:k + 1] + steps[k:k + 1]
        local = merged
    for ci in range(nchunk):
        chain += _scan_chain_steps(lambda i, name, _c=ci: buf[name][rslot + _c * ng + i], ng,
                                   states, ys.append)
    done = 0
    span = max(1, (3 * len(local)) // 4)
    for j, f in enumerate(local):
        f()
        want = min(len(chain), -((j + 1) * len(chain) // -span))
        while done < want:
            chain[done]()
            done += 1
    for ci in range(nchunk):
        y_ref[0, ci * RWKV_CHUNK:(ci + 1) * RWKV_CHUNK, :] = jnp.concatenate(ys[ci], axis=-1)
    for gi in range(ng):
        mt_ref[gi] = states[gi]


def _rwkv_scan(r, lw, k, v, kk, b):
    bsz, t, d = r.shape
    tt = min(ROW_TILE, t)
    nt = t // tt
    ng = d // GROUP_LANES
    slots = 2 * (tt // RWKV_CHUNK) * ng
    c, g = RWKV_CHUNK, GROUP_LANES
    in_spec = pl.BlockSpec((1, tt, d), lambda bi, i: (bi, jnp.minimum(i, nt - 1), 0))
    out_spec = pl.BlockSpec((1, tt, d), lambda bi, i: (bi, jnp.maximum(i - 1, 0), 0))
    shapes = dict(kr=((2 * c, g), BF16), tmat=((c, g), BF16), av=((c, g), F32), arkv=((c, g), F32),
                  arbt=((c + RWKV_HEAD, g), BF16), kv=((RWKV_HEAD, g), F32), pcat=((RWKV_HEAD, g), F32))
    return pl.pallas_call(
        _scan_kernel,
        grid=(bsz, nt + 1),
        in_specs=[in_spec] * 6,
        out_specs=out_spec,
        out_shape=jax.ShapeDtypeStruct(r.shape, F32),
        scratch_shapes=[pltpu.VMEM((ng, RWKV_HEAD, g), F32)]
        + [pltpu.VMEM((slots,) + shapes[n][0], shapes[n][1]) for n in _SCAN_FIELDS],
        compiler_params=_params(),
        name="rwkv_scan",
    )(r, lw, k, v, kk, b)


def _odd_post_kernel(res_ref, ada_ref, y_ref, r_ref, k_ref, v_ref, z_ref, rk_ref, lg_ref, lb_ref,
                     postg_ref, wout_ref, out_ref):
    d = res_ref.shape[2]
    tt = res_ref.shape[1]
    red, spread = _head_reduce_matrix(d), _head_spread_matrix(d)
    inv = 1.0 / RWKV_HEAD
    y = y_ref[0]
    rkr = r_ref[0] * k_ref[0] * rk_ref[...]
    sums = _dot(jnp.concatenate([y, rkr], axis=0), red)
    both = _dot(jnp.concatenate([sums[:tt] * inv, sums[tt:]], axis=0), spread)
    yc = y - both[:tt]
    rstd = lax.rsqrt(_dot(yc * yc, red) * inv + LNX_EPS)
    yn = yc * _dot(rstd, spread) * lg_ref[...] + lb_ref[...]
    out = _dot((yn + both[tt:] * v_ref[0]) * _silu(z_ref[0]), wout_ref[...])
    gate = ada_ref[0][:, 2 * d:]
    out_ref[0] = res_ref[0] + gate * _rms(out, postg_ref[...])


def _odd_post(res, ada, y, r, k, v, z, r_k, lnx_g, lnx_b, post_g, w_out):
    bsz, t, d = res.shape
    tt = min(WIDE_ROW_TILE, t)
    row_spec = pl.BlockSpec((1, tt, d), lambda b, i: (b, i, 0))
    vec = lambda a: a.reshape(1, d)
    return pl.pallas_call(
        _odd_post_kernel,
        grid=(bsz, t // tt),
        in_specs=[row_spec, pl.BlockSpec((1, 1, 3 * d), lambda b, i: (b, 0, 0))] + [row_spec] * 5
        + [_full_spec((1, d))] * 4 + [_full_spec(w_out.shape)],
        out_specs=row_spec,
        out_shape=jax.ShapeDtypeStruct(res.shape, F32),
        compiler_params=_params(),
        name="rwkv_post",
    )(res, ada.reshape(bsz, 1, 3 * d), y, r, k, v, z, vec(r_k), vec(lnx_g), vec(lnx_b), vec(post_g),
      w_out.astype(BF16))


def kernel(x, c, ada_w, ada_b, pre_g, post_g, ev_w_in, ev_w_out, pool_w, pool_scale, hgrn_lb_logits,
           hgrn_onorm_g, rw_mu, rw_w_rkvz, rw_w0, rw_w1, rw_w2, rw_a0, rw_a1, rw_a2, rw_k_k, rw_k_a,
           rw_r_k, rw_lnx_g, rw_lnx_b, rw_w_out, rw_v0, rw_v1, rw_v2):
    depth = ada_w.shape[0]
    res = x.astype(F32)
    ada = _ada_all(c, ada_w, ada_b)
    lb_all = _lower_bounds(hgrn_lb_logits)
    v_first = None
    for layer in range(depth):
        j = layer // 2
        if layer % 2 == 0:
            res = _even_layer(res, ada[layer], pre_g[layer], post_g[layer], ev_w_in[j], ev_w_out[j],
                              pool_w[j], pool_scale[j], lb_all[j], hgrn_onorm_g[j])
        else:
            vfirst = None if v_first is None else (v_first, rw_v0[j - 1], rw_v1[j - 1], rw_v2[j - 1])
            r, lw, k, v, kk, b, z = _odd_pre(res, ada[layer], pre_g[layer], rw_mu[j], rw_w_rkvz[j],
                                             rw_w0[j], rw_w1[j], rw_w2[j], rw_a0[j], rw_a1[j],
                                             rw_a2[j], rw_k_k[j], rw_k_a[j], vfirst)
            if v_first is None:
                v_first = v
            y = _rwkv_scan(r, lw, k, v, kk, b)
            res = _odd_post(res, ada[layer], y, r, k, v, z, rw_r_k[j], rw_lnx_g[j], rw_lnx_b[j],
                            post_g[layer], rw_w_out[j])
    return res.astype(x.dtype)
```

```python
import functools

import jax
import jax.numpy as jnp
from jax import lax
from jax.experimental import pallas as pl
from jax.experimental.pallas import tpu as pltpu

F32 = jnp.float32
BF16 = jnp.bfloat16

NORM_EPS = 1e-6
LOG2_E = 1.4426950408889634
LNX_EPS = 64e-5
POOL_WINDOWS = (2, 4, 8, 16)
MAX_WINDOW = 16
HGRN_HEAD = 128
RWKV_HEAD = 64
RWKV_CHUNK = 64
RWKV_SUB = 16
HEAD_SLOTS = 128
GROUP_LANES = 256
SCAN_CHUNKS_PER_STEP = 2
SCAN_BLOCK_STAGGER = 5
ROW_TILE = 256
WIDE_ROW_TILE = 512
VMEM_LIMIT_BYTES = 60000 * 1024


def _dot(a, b):
    return jnp.dot(a.astype(BF16), b.astype(BF16), preferred_element_type=F32)


def _dot_tb(a, b):
    return lax.dot_general(a.astype(BF16), b.astype(BF16), (((1,), (1,)), ((), ())),
                           preferred_element_type=F32)


def _dot_ta(a, b):
    return lax.dot_general(a.astype(BF16), b.astype(BF16), (((0,), (0,)), ((), ())),
                           preferred_element_type=F32)


def _sigmoid(x):
    return jax.nn.sigmoid(x)


def _silu(x):
    return x * _sigmoid(x)


def _log1pexp_neg_abs(x):
    return jnp.log(1.0 + jnp.exp(-jnp.abs(x)))


def _softplus(x):
    return jnp.maximum(x, 0.0) + _log1pexp_neg_abs(x)


def _rms(x, g):
    return x * lax.rsqrt(jnp.mean(x * x, axis=-1, keepdims=True) + NORM_EPS) * g


def _iota(shape, dim):
    return lax.broadcasted_iota(jnp.int32, shape, dim)


def _segment_cumsum(x, seg):
    tile = 8
    sub = _iota(x.shape, 0) % tile
    sh = 1
    while sh < tile:
        x = x + jnp.where(sub >= sh, pltpu.roll(x, sh, axis=0), 0.0)
        sh *= 2
    per = seg // tile
    out, last, carry = [], [], None
    for j in range(x.shape[0] // tile):
        blk = x[j * tile:(j + 1) * tile, :]
        tot = blk[tile - 1:tile, :]
        if j % per:
            blk, tot = blk + carry, tot + carry
        out.append(blk)
        carry = tot
        if j % per == per - 1:
            last += [jnp.broadcast_to(tot, (seg, x.shape[1]))]
    return jnp.concatenate(out, axis=0), jnp.concatenate(last, axis=0)


def _full_spec(shape, single_buffer=False):
    nd = len(shape)
    mode = pl.Buffered(1) if single_buffer else None
    return pl.BlockSpec(shape, lambda *_: (0,) * nd, pipeline_mode=mode)


def _params(**flags):
    return pltpu.CompilerParams(dimension_semantics=("arbitrary", "arbitrary"),
                                vmem_limit_bytes=VMEM_LIMIT_BYTES, flags=flags or None)


def _ada_kernel(c_ref, w_ref, b_ref, o_ref):
    cond = _silu(c_ref[...].astype(F32))
    o_ref[0] = jnp.dot(cond, w_ref[0].astype(F32), preferred_element_type=F32,
                       precision=lax.Precision.HIGHEST) + b_ref[0]


def _ada_all(c, ada_w, ada_b):
    depth, d, d3 = ada_w.shape
    bsz = c.shape[0]
    tn = 1024
    return pl.pallas_call(
        _ada_kernel,
        grid=(depth, d3 // tn),
        in_specs=[pl.BlockSpec((bsz, d), lambda l, n: (0, 0)),
                  pl.BlockSpec((1, d, tn), lambda l, n: (l, 0, n)),
                  pl.BlockSpec((1, 1, tn), lambda l, n: (l, 0, n))],
        out_specs=pl.BlockSpec((1, bsz, tn), lambda l, n: (l, 0, n)),
        out_shape=jax.ShapeDtypeStruct((depth, bsz, d3), F32),
        compiler_params=_params(),
        name="ada_ln",
    )(c, ada_w, ada_b.reshape(depth, 1, d3))


def _lb_kernel(logit_ref, o_ref):
    x = logit_ref[...].astype(F32)
    n = x.shape[0]
    m = jnp.max(x, axis=0, keepdims=True)
    e = jnp.exp(x - m)
    sm = e / jnp.sum(e, axis=0, keepdims=True)
    acc = jnp.zeros_like(sm[0:1])
    for i in range(n):
        if i > 0:
            acc = acc + sm[i:i + 1]
        o_ref[i:i + 1, :] = acc


def _lower_bounds(logits):
    return pl.pallas_call(
        _lb_kernel,
        out_shape=jax.ShapeDtypeStruct(logits.shape, F32),
        name="hgrn_lower_bounds",
    )(logits)


def _even_kernel(res_ref, ada_ref, preg_ref, postg_ref, win_ref, wout_ref, poolw_ref, pscale_ref,
                 lb_ref, og_ref, out_ref,
                 pbuf, st_ref, hq_ref, hk_ref, hc_ref, hv_ref, ho_ref):
    ti = pl.program_id(1)
    tt, d = res_ref.shape[1], res_ref.shape[2]
    pw = pbuf.shape[1]
    nh = pw // HGRN_HEAD
    sub = MAX_WINDOW
    pad = MAX_WINDOW

    @pl.when(ti == 0)
    def _():
        pbuf[0:pad, :] = jnp.zeros((pad, pw), F32)
        st_ref[...] = jnp.zeros(st_ref.shape, F32)

    x = res_ref[0]
    ada = ada_ref[0]
    shift, scale, gate = ada[:, :d], ada[:, d:2 * d], ada[:, 2 * d:]
    h = _rms(x, preg_ref[...]) * (1.0 + scale) + shift
    u = _dot(h, win_ref[...])

    row = _iota((tt, 1), 0)
    up = u[:, :pw]
    pbuf[pad:pad + tt, :] = up
    pos1 = (ti * tt + row + 1).astype(F32)
    gw = pw // len(POOL_WINDOWS)
    pooled = []
    for gi, win in enumerate(POOL_WINDOWS):
        sl = slice(gi * gw, (gi + 1) * gw)
        s = pbuf[:, sl]
        sh = 1
        while sh < win:
            s = s + pltpu.roll(s, sh, axis=0)
            sh *= 2
        p = s[pad:, :] / jnp.minimum(pos1, float(win)) - up[:, sl]
        pooled.append(_dot(p, poolw_ref[gi]))
    pbuf[0:pad, :] = pbuf[tt:tt + pad, :]
    y_pool = jnp.concatenate(pooled, axis=-1) * pscale_ref[...]

    q = _silu(u[:, pw:2 * pw])
    fr = u[:, 2 * pw:3 * pw]
    v = u[:, 3 * pw:4 * pw]
    z = u[:, 4 * pw:]
    lb = lb_ref[...]
    k = (1.0 - lb) * _sigmoid(-fr)
    la = jnp.log(lb)
    lsig = jnp.minimum(fr, 0.0) - _log1pexp_neg_abs(fr)
    lbb = jnp.log1p(-lb) + lsig
    logf = jnp.maximum(la, lbb) + _log1pexp_neg_abs(la - lbb)

    cs, csl = _segment_cumsum(logf, sub)
    qt = q * jnp.exp(cs)
    kvec = k * jnp.exp(csl - cs)
    dl = jnp.exp(csl)
    nsub = tt // sub
    heads = [slice(hh * HGRN_HEAD, (hh + 1) * HGRN_HEAD) for hh in range(nh)]
    subs = [slice(m * sub, (m + 1) * sub) for m in range(nsub)]

    upd = [[_dot_ta(v[rows, hs], kvec[rows, hs]) for hs in heads] for rows in subs]
    start = []
    states = [st_ref[hh] for hh in range(nh)]
    for m, rows in enumerate(subs):
        start.append([s.astype(BF16) for s in states])
        states = [s * dl[m * sub:m * sub + 1, hs] + upd[m][hh]
                  for hh, (s, hs) in enumerate(zip(states, heads))]
    for hh in range(nh):
        st_ref[hh] = states[hh]
    o_inter = [jnp.concatenate([_dot_tb(qt[rows, hs], start[m][hh]) for m, rows in enumerate(subs)],
                               axis=0) for hh, hs in enumerate(heads)]

    for hh, hs in enumerate(heads):
        hq_ref[hh] = q[:, hs]
        hk_ref[hh] = k[:, hs]
        hc_ref[hh] = cs[:, hs] * LOG2_E
        hv_ref[hh] = v[:, hs]
    o_heads = []
    for hh in range(nh):
        at = lambda ref, t: ref[hh, pl.ds(t, nsub, stride=sub), :]
        qs = [at(hq_ref, t) for t in range(sub)]
        ks = [at(hk_ref, t) for t in range(sub)]
        cc = [at(hc_ref, t) for t in range(sub)]
        vv = [at(hv_ref, t) for t in range(sub)]
        for t in range(sub):
            acc = jnp.sum(qs[t] * ks[t], axis=-1, keepdims=True) * vv[t]
            for s in range(t):
                p = qs[t] * ks[s] * jnp.exp2(cc[t] - cc[s])
                acc = acc + jnp.sum(p, axis=-1, keepdims=True) * vv[s]
            ho_ref[hh, pl.ds(t, nsub, stride=sub), :] = acc
        o_heads.append(o_inter[hh] + ho_ref[hh])
    og = og_ref[...]
    y_h = [o * lax.rsqrt(jnp.mean(o * o, axis=-1, keepdims=True) + NORM_EPS)
           * og[:, hh * HGRN_HEAD:(hh + 1) * HGRN_HEAD] for hh, o in enumerate(o_heads)]

    y = jnp.concatenate([y_pool] + y_h, axis=-1) * _silu(z)
    out = _dot(y, wout_ref[...])
    out_ref[0] = x + gate * _rms(out, postg_ref[...])


def _even_layer(res, ada, pre_g, post_g, w_in, w_out, pool_w, pool_scale, lb, onorm_g):
    bsz, t, d = res.shape
    tt = min(WIDE_ROW_TILE, t)
    pw = pool_scale.shape[-1]
    nh = pw // HGRN_HEAD
    row_spec = pl.BlockSpec((1, tt, d), lambda b, i: (b, i, 0))
    return pl.pallas_call(
        _even_kernel,
        grid=(bsz, t // tt),
        in_specs=[row_spec,
                  pl.BlockSpec((1, 1, 3 * d), lambda b, i: (b, 0, 0)),
                  _full_spec((1, d)), _full_spec((1, d)),
                  _full_spec(w_in.shape), _full_spec(w_out.shape), _full_spec(pool_w.shape),
                  _full_spec((1, pw)), _full_spec((1, pw)), _full_spec((1, pw))],
        out_specs=row_spec,
        out_shape=jax.ShapeDtypeStruct(res.shape, F32),
        scratch_shapes=[pltpu.VMEM((MAX_WINDOW + tt, pw), F32),
                        pltpu.VMEM((nh, HGRN_HEAD, HGRN_HEAD), F32)]
        + [pltpu.VMEM((nh, tt, HGRN_HEAD), F32) for _ in range(5)],
        compiler_params=_params(),
        name="even_layer",
    )(res, ada.reshape(bsz, 1, 3 * d), pre_g.reshape(1, d), post_g.reshape(1, d),
      w_in.astype(BF16), w_out.astype(BF16), pool_w.astype(BF16),
      pool_scale.reshape(1, pw), lb.reshape(1, pw), onorm_g.reshape(1, pw))


def _head_reduce_matrix(d):
    return jnp.where(_iota((d, HEAD_SLOTS), 0) // RWKV_HEAD == _iota((d, HEAD_SLOTS), 1), 1.0, 0.0).astype(BF16)


def _head_spread_matrix(d):
    return jnp.where(_iota((HEAD_SLOTS, d), 0) == _iota((HEAD_SLOTS, d), 1) // RWKV_HEAD, 1.0, 0.0).astype(BF16)


def _odd_pre_kernel(has_vfirst, *refs):
    if has_vfirst:
        (res_ref, ada_ref, preg_ref, mu_ref, w_ref, w0_ref, w1_ref, w2_ref, a0_ref, a1_ref, a2_ref,
         kk_ref, ka_ref, vf_ref, v0_ref, v1_ref, v2_ref,
         r_out, lw_out, k_out, v_out, kk_out, b_out, z_out, prev_ref) = refs
    else:
        (res_ref, ada_ref, preg_ref, mu_ref, w_ref, w0_ref, w1_ref, w2_ref, a0_ref, a1_ref, a2_ref,
         kk_ref, ka_ref,
         r_out, lw_out, k_out, v_out, kk_out, b_out, z_out, prev_ref) = refs
    ti = pl.program_id(1)
    tt, d = res_ref.shape[1], res_ref.shape[2]

    @pl.when(ti == 0)
    def _():
        prev_ref[...] = jnp.zeros(prev_ref.shape, F32)

    x = res_ref[0]
    ada = ada_ref[0]
    shift, scale = ada[:, :d], ada[:, d:2 * d]
    h = _rms(x, preg_ref[...]) * (1.0 + scale) + shift
    row = _iota((tt, 1), 0)
    hs = jnp.where(row == 0, prev_ref[0:1, :], pltpu.roll(h, 1, axis=0))
    prev_ref[0:1, :] = h[tt - 1:tt, :]
    xx = hs - h
    xm = lambda p: h + xx * mu_ref[p:p + 1, :]

    r = _dot(xm(0), w_ref[0])
    k = _dot(xm(1), w_ref[1])
    xv = xm(2)
    v = _dot(xv, w_ref[2])
    z_out[0] = _dot(xm(3), w_ref[3])
    logw = -_softplus(-(w0_ref[...] + _dot(jnp.tanh(_dot(xm(4), w1_ref[...])), w2_ref[...]))) - 0.5
    lw_out[0] = -jnp.exp(logw)
    a = _sigmoid(a0_ref[...] + _dot(_dot(xm(5), a1_ref[...]), a2_ref[...]))
    if has_vfirst:
        v = v + (vf_ref[0] - v) * _sigmoid(v0_ref[...] + _dot(_dot(xv, v1_ref[...]), v2_ref[...]))
    kkr = k * kk_ref[...]
    ss = _dot(kkr * kkr, _head_reduce_matrix(d))
    kk = kkr * _dot(1.0 / jnp.maximum(jnp.sqrt(ss), 1e-12), _head_spread_matrix(d))
    r_out[0] = r
    k_out[0] = k * (1.0 + (a - 1.0) * ka_ref[...])
    v_out[0] = v
    kk_out[0] = kk
    b_out[0] = kk * a


def _odd_pre(res, ada, pre_g, mu, w_rkvz, w0, w1, w2, a0, a1, a2, k_k, k_a, vfirst=None):
    bsz, t, d = res.shape
    tt = min(WIDE_ROW_TILE, t)
    row_spec = pl.BlockSpec((1, tt, d), lambda b, i: (b, i, 0))
    vec = lambda a: a.reshape(1, d)
    args = [res, ada.reshape(bsz, 1, 3 * d), vec(pre_g), mu, w_rkvz.astype(BF16), vec(w0),
            w1.astype(BF16), w2.astype(BF16), vec(a0), a1.astype(BF16), a2.astype(BF16),
            vec(k_k), vec(k_a)]
    specs = [row_spec, pl.BlockSpec((1, 1, 3 * d), lambda b, i: (b, 0, 0)), _full_spec((1, d)),
             _full_spec(mu.shape), _full_spec(w_rkvz.shape, single_buffer=True), _full_spec((1, d)),
             _full_spec(w1.shape), _full_spec(w2.shape), _full_spec((1, d)),
             _full_spec(a1.shape), _full_spec(a2.shape), _full_spec((1, d)), _full_spec((1, d))]
    if vfirst is not None:
        v_first, v0, v1, v2 = vfirst
        args += [v_first, vec(v0), v1.astype(BF16), v2.astype(BF16)]
        specs += [row_spec, _full_spec((1, d)), _full_spec(v1.shape), _full_spec(v2.shape)]
    return pl.pallas_call(
        functools.partial(_odd_pre_kernel, vfirst is not None),
        grid=(bsz, t // tt),
        in_specs=specs,
        out_specs=[row_spec] * 7,
        out_shape=[jax.ShapeDtypeStruct(res.shape, F32)] * 7,
        scratch_shapes=[pltpu.VMEM((8, d), F32)],
        compiler_params=_params(),
        name="rwkv_pre",
    )(*args)


def _block_rows(y, n_heads, head):
    yb = y.astype(BF16)
    lane_head = _iota(yb.shape, 1) // head
    return jnp.concatenate([jnp.where(lane_head == hh, yb, jnp.zeros_like(yb)) for hh in range(n_heads)],
                           axis=0)


def _head_transpose(x, n_heads):
    w = x.shape[1] // n_heads
    return jnp.concatenate([x[:, hh * w:(hh + 1) * w] for hh in range(n_heads)], axis=0).T


def _scan_local_steps(ins, store):
    rs, lws, ks, vs, kks, bs = ins
    c_len, g = rs[0].shape
    nh = g // RWKV_HEAD
    hd = RWKV_HEAD
    each = lambda f, *ls: [f(*xs) for xs in zip(*ls)]
    bd = lambda y: _block_rows(y, nh, hd)
    mm = lambda x, y: _dot(x, bd(y))
    stack = lambda x, y: jnp.concatenate([x.astype(BF16), y.astype(BF16)], axis=0)
    wc = nh * c_len

    t_idx = _iota((c_len, wc), 0)
    s_idx = _iota((c_len, wc), 1) % c_len
    strict = s_idx < t_idx
    incl = s_idx <= t_idx
    diag_blk = strict & ((s_idx // RWKV_SUB) == (t_idx // RWKV_SUB))
    eye = jnp.where(s_idx == t_idx, 1.0, 0.0)
    dia = _iota((c_len, hd), 0) == _iota((c_len, hd), 1)

    e = {}
    steps = []

    def step(f):
        steps.append(f)
        return f

    @step
    def _():
        e["cs"] = each(lambda lw: _segment_cumsum(lw, c_len)[0], lws)

    @step
    def _():
        cs = e["cs"]
        cl = [c[c_len - 1:c_len, :] for c in cs]
        e["rt"] = each(lambda r, c: r * jnp.exp(c), rs, cs)
        e["kkt"] = each(lambda kk, c, lw: kk * jnp.exp(c - lw), kks, cs, lws)
        e["kh"] = each(lambda k, c: k * jnp.exp(-c), ks, cs)
        e["bh"] = each(lambda b, c: b * jnp.exp(-c), bs, cs)
        e["pend"] = [jnp.exp(l) for l in cl]
        e["kvec"] = each(lambda kh, p: kh * p, e["kh"], e["pend"])
        e["bvec"] = each(lambda bh, p: bh * p, e["bh"], e["pend"])

    @step
    def _():
        a_all = each(lambda kkt, rt, bh, kh: _dot_tb(stack(kkt, rt), stack(bd(bh), bd(kh))),
                     e["kkt"], e["rt"], e["bh"], e["kh"])
        a_kb = [a[:c_len, :wc] for a in a_all]
        e["a_kk"] = [jnp.where(strict, a[:c_len, wc:], 0.0) for a in a_all]
        e["a_rb"] = [jnp.where(incl, a[c_len:, :wc], 0.0) for a in a_all]
        e["a_rk"] = [jnp.where(incl, a[c_len:, wc:], 0.0) for a in a_all]
        e["n_p"] = each(lambda a: jnp.where(diag_blk, a, 0.0), a_kb)
        e["n_o"] = each(lambda a: jnp.where(strict & ~diag_blk, a, 0.0), a_kb)
        e["t_d"] = each(lambda n: eye - n, e["n_p"])

    @step
    def _():
        e["n_p"] = each(mm, e["n_p"], e["n_p"])

    for _ in range(RWKV_SUB.bit_length() - 3):
        @step
        def _():
            both = each(lambda t, n: _dot(stack(t, n), bd(n)), e["t_d"], e["n_p"])
            e["t_d"] = each(lambda t, s: t + s[:c_len], e["t_d"], both)
            e["n_p"] = [s[c_len:] for s in both]

    @step
    def _():
        e["t_d"] = each(lambda t, n: t + mm(t, n), e["t_d"], e["n_p"])

    @step
    def _():
        e["z_p"] = each(mm, e["t_d"], e["n_o"])
        e["t_m"] = each(lambda z: eye - z, e["z_p"])

    for _ in range((c_len // RWKV_SUB).bit_length() - 2):
        @step
        def _():
            e["z_p"] = each(mm, e["z_p"], e["z_p"])

        @step
        def _():
            e["t_m"] = each(lambda t, z: t + mm(t, z), e["t_m"], e["z_p"])

    @step
    def _():
        e["tmat"] = each(mm, e["t_m"], e["t_d"])

    @step
    def _():
        e["av"] = each(lambda akk, ark, kv, v: _dot(
            jnp.concatenate([akk.astype(BF16), ark.astype(BF16), _head_transpose(kv, nh).astype(BF16)], axis=0),
            bd(v)), e["a_kk"], e["a_rk"], e["kvec"], vs)

    @step
    def _():
        for i in range(len(rs)):
            pcat = jnp.concatenate(
                [jnp.broadcast_to(jnp.sum(jnp.where(dia, e["pend"][i][:, hh * hd:(hh + 1) * hd], 0.0),
                                          axis=1, keepdims=True), (c_len, hd)) for hh in range(nh)], axis=1)
            av = e["av"][i]
            store(i, kr=stack(e["kkt"][i], e["rt"][i]), tmat=e["tmat"][i].astype(BF16),
                  av=av[:c_len], arkv=av[c_len:2 * c_len], kv=av[2 * c_len:],
                  arbt=stack(e["a_rb"][i], _head_transpose(e["bvec"][i], nh)), pcat=pcat)

    return steps


def _scan_chain_steps(load, n_inst, states, emit):
    g = states[0].shape[1]
    nh = g // RWKV_HEAD
    c_len = RWKV_CHUNK
    bd = lambda y: _block_rows(y, nh, RWKV_HEAD)
    e = {}

    def s1():
        e["x"] = [_dot(load(i, "kr"), bd(states[i])) for i in range(n_inst)]

    def s2():
        e["u"] = [_dot(load(i, "tmat"), bd(e["x"][i][:c_len] + load(i, "av"))) for i in range(n_inst)]

    def s3():
        prod = [_dot(load(i, "arbt"), bd(e["u"][i])) for i in range(n_inst)]
        emit([e["x"][i][c_len:] + load(i, "arkv") - prod[i][:c_len] for i in range(n_inst)])
        for i in range(n_inst):
            states[i] = states[i] * load(i, "pcat") + load(i, "kv") - prod[i][c_len:]

    return [s1, s2, s3]


_SCAN_FIELDS = ("kr", "tmat", "av", "arkv", "arbt", "kv", "pcat")


def _scan_kernel(tiles_per_seq, r_ref, lw_ref, k_ref, v_ref, kk_ref, b_ref, y_ref, mt_ref, *bufs):
    ti = pl.program_id(1)
    tt, d = r_ref.shape[1], r_ref.shape[2]
    ng = d // GROUP_LANES
    nchunk = tt // RWKV_CHUNK
    ninst = nchunk * ng
    in_refs = (r_ref, lw_ref, k_ref, v_ref, kk_ref, b_ref)
    buf = dict(zip(_SCAN_FIELDS, bufs))
    wslot = (ti % 2) * ninst
    rslot = ((ti + 1) % 2) * ninst

    @pl.when(ti == 0)
    def _():
        mt_ref[...] = jnp.zeros(mt_ref.shape, F32)
        for name in _SCAN_FIELDS:
            ref = buf[name]
            ref[pl.ds(ninst, ninst)] = jnp.zeros((ninst,) + ref.shape[1:], ref.dtype)

    def store(i, **fields):
        for name, val in fields.items():
            buf[name][wslot + i] = val

    keep = jnp.where((ti + tiles_per_seq - 1) % tiles_per_seq == 0, 0.0, 1.0)
    states = [mt_ref[gi] * keep for gi in range(ng)]
    ys = []
    blocks, chain = [], []
    per_block = SCAN_CHUNKS_PER_STEP * ng
    for b0 in range(0, ninst, per_block):
        idx = range(b0, b0 + per_block)
        ins = [[ref[0, (i // ng) * RWKV_CHUNK:(i // ng + 1) * RWKV_CHUNK,
                    (i % ng) * GROUP_LANES:(i % ng + 1) * GROUP_LANES] for i in idx]
               for ref in in_refs]
        blocks.append(_scan_local_steps(ins, lambda i, _b0=b0, **f: store(_b0 + i, **f)))
    local = []
    for bi, steps in enumerate(blocks):
        at = min(len(local), bi * SCAN_BLOCK_STAGGER)
        merged = local[:at]
        rest = local[at:]
        for k in range(max(len(rest), len(steps))):
            merged += rest[k:k + 1] + steps[k:k + 1]
        local = merged
    for ci in range(nchunk):
        chain += _scan_chain_steps(lambda i, name, _c=ci: buf[name][rslot + _c * ng + i], ng,
                                   states, ys.append)
    done = 0
    span = max(1, (3 * len(local)) // 4)
    for j, f in enumerate(local):
        f()
        want = min(len(chain), -((j + 1) * len(chain) // -span))
        while done < want:
            chain[done]()
            done += 1
    for ci in range(nchunk):
        y_ref[0, ci * RWKV_CHUNK:(ci + 1) * RWKV_CHUNK, :] = jnp.concatenate(ys[ci], axis=-1)
    for gi in range(ng):
        mt_ref[gi] = states[gi]


def _rwkv_scan(r, lw, k, v, kk, b):
    bsz, t, d = r.shape
    tt = min(ROW_TILE, t)
    nt = bsz * (t // tt)
    ng = d // GROUP_LANES
    slots = 2 * (tt // RWKV_CHUNK) * ng
    c, g = RWKV_CHUNK, GROUP_LANES
    in_spec = pl.BlockSpec((1, tt, d), lambda bi, i: (bi, jnp.minimum(i, nt - 1), 0))
    out_spec = pl.BlockSpec((1, tt, d), lambda bi, i: (bi, jnp.maximum(i - 1, 0), 0))
    flat = lambda a: a.reshape(1, bsz * t, d)
    shapes = dict(kr=((2 * c, g), BF16), tmat=((c, g), BF16), av=((c, g), F32), arkv=((c, g), F32),
                  arbt=((c + RWKV_HEAD, g), BF16), kv=((RWKV_HEAD, g), F32), pcat=((RWKV_HEAD, g), F32))
    return pl.pallas_call(
        functools.partial(_scan_kernel, t // tt),
        grid=(1, nt + 1),
        in_specs=[in_spec] * 6,
        out_specs=out_spec,
        out_shape=jax.ShapeDtypeStruct((1, bsz * t, d), F32),
        scratch_shapes=[pltpu.VMEM((ng, RWKV_HEAD, g), F32)]
        + [pltpu.VMEM((slots,) + shapes[n][0], shapes[n][1]) for n in _SCAN_FIELDS],
        compiler_params=_params(),
        name="rwkv_scan",
    )(flat(r), flat(lw), flat(k), flat(v), flat(kk), flat(b)).reshape(bsz, t, d)


def _odd_post_kernel(res_ref, ada_ref, y_ref, r_ref, k_ref, v_ref, z_ref, rk_ref, lg_ref, lb_ref,
                     postg_ref, wout_ref, out_ref):
    d = res_ref.shape[2]
    tt = res_ref.shape[1]
    red, spread = _head_reduce_matrix(d), _head_spread_matrix(d)
    inv = 1.0 / RWKV_HEAD
    y = y_ref[0]
    rkr = r_ref[0] * k_ref[0] * rk_ref[...]
    sums = _dot(jnp.concatenate([y, rkr], axis=0), red)
    both = _dot(jnp.concatenate([sums[:tt] * inv, sums[tt:]], axis=0), spread)
    yc = y - both[:tt]
    rstd = lax.rsqrt(_dot(yc * yc, red) * inv + LNX_EPS)
    yn = yc * _dot(rstd, spread) * lg_ref[...] + lb_ref[...]
    out = _dot((yn + both[tt:] * v_ref[0]) * _silu(z_ref[0]), wout_ref[...])
    gate = ada_ref[0][:, 2 * d:]
    out_ref[0] = res_ref[0] + gate * _rms(out, postg_ref[...])


def _odd_post(res, ada, y, r, k, v, z, r_k, lnx_g, lnx_b, post_g, w_out):
    bsz, t, d = res.shape
    tt = min(WIDE_ROW_TILE, t)
    row_spec = pl.BlockSpec((1, tt, d), lambda b, i: (b, i, 0))
    vec = lambda a: a.reshape(1, d)
    return pl.pallas_call(
        _odd_post_kernel,
        grid=(bsz, t // tt),
        in_specs=[row_spec, pl.BlockSpec((1, 1, 3 * d), lambda b, i: (b, 0, 0))] + [row_spec] * 5
        + [_full_spec((1, d))] * 4 + [_full_spec(w_out.shape)],
        out_specs=row_spec,
        out_shape=jax.ShapeDtypeStruct(res.shape, F32),
        compiler_params=_params(),
        name="rwkv_post",
    )(res, ada.reshape(bsz, 1, 3 * d), y, r, k, v, z, vec(r_k), vec(lnx_g), vec(lnx_b), vec(post_g),
      w_out.astype(BF16))


def kernel(x, c, ada_w, ada_b, pre_g, post_g, ev_w_in, ev_w_out, pool_w, pool_scale, hgrn_lb_logits,
           hgrn_onorm_g, rw_mu, rw_w_rkvz, rw_w0, rw_w1, rw_w2, rw_a0, rw_a1, rw_a2, rw_k_k, rw_k_a,
           rw_r_k, rw_lnx_g, rw_lnx_b, rw_w_out, rw_v0, rw_v1, rw_v2):
    depth = ada_w.shape[0]
    res = x.astype(F32)
    ada = _ada_all(c, ada_w, ada_b)
    lb_all = _lower_bounds(hgrn_lb_logits)
    v_first = None
    for layer in range(depth):
        j = layer // 2
        if layer % 2 == 0:
            res = _even_layer(res, ada[layer], pre_g[layer], post_g[layer], ev_w_in[j], ev_w_out[j],
                              pool_w[j], pool_scale[j], lb_all[j], hgrn_onorm_g[j])
        else:
            vfirst = None if v_first is None else (v_first, rw_v0[j - 1], rw_v1[j - 1], rw_v2[j - 1])
            r, lw, k, v, kk, b, z = _odd_pre(res, ada[layer], pre_g[layer], rw_mu[j], rw_w_rkvz[j],
                                             rw_w0[j], rw_w1[j], rw_w2[j], rw_a0[j], rw_a1[j],
                                             rw_a2[j], rw_k_k[j], rw_k_a[j], vfirst)
            if v_first is None:
                v_first = v
            y = _rwkv_scan(r, lw, k, v, kk, b)
            res = _odd_post(res, ada[layer], y, r, k, v, z, rw_r_k[j], rw_lnx_g[j], rw_lnx_b[j],
                            post_g[layer], rw_w_out[j])
    return res.astype(x.dtype)
```

```python
import functools

import jax
import jax.numpy as jnp
from jax import lax
from jax.experimental import pallas as pl
from jax.experimental.pallas import tpu as pltpu

F32 = jnp.float32
BF16 = jnp.bfloat16

NORM_EPS = 1e-6
LOG2_E = 1.4426950408889634
LNX_EPS = 64e-5
POOL_WINDOWS = (2, 4, 8, 16)
MAX_WINDOW = 16
HGRN_HEAD = 128
RWKV_HEAD = 64
RWKV_CHUNK = 64
RWKV_SUB = 16
HEAD_SLOTS = 128
GROUP_LANES = 256
SCAN_CHUNKS_PER_STEP = 2
SCAN_BLOCK_STAGGER = 5
ROW_TILE = 256
WIDE_ROW_TILE = 512
VMEM_LIMIT_BYTES = 60000 * 1024


def _dot(a, b):
    return jnp.dot(a.astype(BF16), b.astype(BF16), preferred_element_type=F32)


def _dot_tb(a, b):
    return lax.dot_general(a.astype(BF16), b.astype(BF16), (((1,), (1,)), ((), ())),
                           preferred_element_type=F32)


def _dot_ta(a, b):
    return lax.dot_general(a.astype(BF16), b.astype(BF16), (((0,), (0,)), ((), ())),
                           preferred_element_type=F32)


def _sigmoid(x):
    return jax.nn.sigmoid(x)


def _silu(x):
    return x * _sigmoid(x)


def _log1pexp_neg_abs(x):
    return jnp.log(1.0 + jnp.exp(-jnp.abs(x)))


def _softplus(x):
    return jnp.maximum(x, 0.0) + _log1pexp_neg_abs(x)


def _rms(x, g):
    return x * lax.rsqrt(jnp.mean(x * x, axis=-1, keepdims=True) + NORM_EPS) * g


def _iota(shape, dim):
    return lax.broadcasted_iota(jnp.int32, shape, dim)


def _segment_cumsum(x, seg):
    tile = 8
    sub = _iota(x.shape, 0) % tile
    sh = 1
    while sh < tile:
        x = x + jnp.where(sub >= sh, pltpu.roll(x, sh, axis=0), 0.0)
        sh *= 2
    per = seg // tile
    out, last, carry = [], [], None
    for j in range(x.shape[0] // tile):
        blk = x[j * tile:(j + 1) * tile, :]
        tot = blk[tile - 1:tile, :]
        if j % per:
            blk, tot = blk + carry, tot + carry
        out.append(blk)
        carry = tot
        if j % per == per - 1:
            last += [jnp.broadcast_to(tot, (seg, x.shape[1]))]
    return jnp.concatenate(out, axis=0), jnp.concatenate(last, axis=0)


def _full_spec(shape, single_buffer=False):
    nd = len(shape)
    mode = pl.Buffered(1) if single_buffer else None
    return pl.BlockSpec(shape, lambda *_: (0,) * nd, pipeline_mode=mode)


def _params(**flags):
    return pltpu.CompilerParams(dimension_semantics=("arbitrary", "arbitrary"),
                                vmem_limit_bytes=VMEM_LIMIT_BYTES, flags=flags or None)


def _ada_kernel(c_ref, w_ref, b_ref, o_ref):
    cond = _silu(c_ref[...].astype(F32))
    o_ref[0] = jnp.dot(cond, w_ref[0].astype(F32), preferred_element_type=F32,
                       precision=lax.Precision.HIGHEST) + b_ref[0]


def _ada_all(c, ada_w, ada_b):
    depth, d, d3 = ada_w.shape
    bsz = c.shape[0]
    tn = 1024
    return pl.pallas_call(
        _ada_kernel,
        grid=(depth, d3 // tn),
        in_specs=[pl.BlockSpec((bsz, d), lambda l, n: (0, 0)),
                  pl.BlockSpec((1, d, tn), lambda l, n: (l, 0, n)),
                  pl.BlockSpec((1, 1, tn), lambda l, n: (l, 0, n))],
        out_specs=pl.BlockSpec((1, bsz, tn), lambda l, n: (l, 0, n)),
        out_shape=jax.ShapeDtypeStruct((depth, bsz, d3), F32),
        compiler_params=_params(),
        name="ada_ln",
    )(c, ada_w, ada_b.reshape(depth, 1, d3))


def _lb_kernel(logit_ref, o_ref):
    x = logit_ref[...].astype(F32)
    n = x.shape[0]
    m = jnp.max(x, axis=0, keepdims=True)
    e = jnp.exp(x - m)
    sm = e / jnp.sum(e, axis=0, keepdims=True)
    acc = jnp.zeros_like(sm[0:1])
    for i in range(n):
        if i > 0:
            acc = acc + sm[i:i + 1]
        o_ref[i:i + 1, :] = acc


def _lower_bounds(logits):
    return pl.pallas_call(
        _lb_kernel,
        out_shape=jax.ShapeDtypeStruct(logits.shape, F32),
        name="hgrn_lower_bounds",
    )(logits)


def _even_kernel(res_ref, ada_ref, preg_ref, postg_ref, win_ref, wout_ref, poolw_ref, pscale_ref,
                 lb_ref, og_ref, out_ref,
                 pbuf, st_ref, hq_ref, hk_ref, hc_ref, hv_ref, ho_ref):
    ti = pl.program_id(1)
    tt, d = res_ref.shape[1], res_ref.shape[2]
    pw = pbuf.shape[1]
    nh = pw // HGRN_HEAD
    sub = MAX_WINDOW
    pad = MAX_WINDOW

    @pl.when(ti == 0)
    def _():
        pbuf[0:pad, :] = jnp.zeros((pad, pw), F32)
        st_ref[...] = jnp.zeros(st_ref.shape, F32)

    x = res_ref[0]
    ada = ada_ref[0]
    shift, scale, gate = ada[:, :d], ada[:, d:2 * d], ada[:, 2 * d:]
    h = _rms(x, preg_ref[...]) * (1.0 + scale) + shift
    u = _dot(h, win_ref[...])

    row = _iota((tt, 1), 0)
    up = u[:, :pw]
    pbuf[pad:pad + tt, :] = up
    pos1 = (ti * tt + row + 1).astype(F32)
    gw = pw // len(POOL_WINDOWS)
    pooled = []
    for gi, win in enumerate(POOL_WINDOWS):
        sl = slice(gi * gw, (gi + 1) * gw)
        s = pbuf[:, sl]
        sh = 1
        while sh < win:
            s = s + pltpu.roll(s, sh, axis=0)
            sh *= 2
        p = s[pad:, :] / jnp.minimum(pos1, float(win)) - up[:, sl]
        pooled.append(_dot(p, poolw_ref[gi]))
    pbuf[0:pad, :] = pbuf[tt:tt + pad, :]
    y_pool = jnp.concatenate(pooled, axis=-1) * pscale_ref[...]

    q = _silu(u[:, pw:2 * pw])
    fr = u[:, 2 * pw:3 * pw]
    v = u[:, 3 * pw:4 * pw]
    z = u[:, 4 * pw:]
    lb = lb_ref[...]
    k = (1.0 - lb) * _sigmoid(-fr)
    la = jnp.log(lb)
    lsig = jnp.minimum(fr, 0.0) - _log1pexp_neg_abs(fr)
    lbb = jnp.log1p(-lb) + lsig
    logf = jnp.maximum(la, lbb) + _log1pexp_neg_abs(la - lbb)

    cs, csl = _segment_cumsum(logf, sub)
    qt = q * jnp.exp(cs)
    kvec = k * jnp.exp(csl - cs)
    dl = jnp.exp(csl)
    nsub = tt // sub
    heads = [slice(hh * HGRN_HEAD, (hh + 1) * HGRN_HEAD) for hh in range(nh)]
    subs = [slice(m * sub, (m + 1) * sub) for m in range(nsub)]

    upd = [[_dot_ta(v[rows, hs], kvec[rows, hs]) for hs in heads] for rows in subs]
    start = []
    states = [st_ref[hh] for hh in range(nh)]
    for m, rows in enumerate(subs):
        start.append([s.astype(BF16) for s in states])
        states = [s * dl[m * sub:m * sub + 1, hs] + upd[m][hh]
                  for hh, (s, hs) in enumerate(zip(states, heads))]
    for hh in range(nh):
        st_ref[hh] = states[hh]
    o_inter = [jnp.concatenate([_dot_tb(qt[rows, hs], start[m][hh]) for m, rows in enumerate(subs)],
                               axis=0) for hh, hs in enumerate(heads)]

    for hh, hs in enumerate(heads):
        hq_ref[hh] = q[:, hs]
        hk_ref[hh] = k[:, hs]
        hc_ref[hh] = cs[:, hs] * LOG2_E
        hv_ref[hh] = v[:, hs]
    o_heads = []
    for hh in range(nh):
        at = lambda ref, t: ref[hh, pl.ds(t, nsub, stride=sub), :]
        qs = [at(hq_ref, t) for t in range(sub)]
        ks = [at(hk_ref, t) for t in range(sub)]
        cc = [at(hc_ref, t) for t in range(sub)]
        vv = [at(hv_ref, t) for t in range(sub)]
        for t in range(sub):
            acc = jnp.sum(qs[t] * ks[t], axis=-1, keepdims=True) * vv[t]
            for s in range(t):
                p = qs[t] * ks[s] * jnp.exp2(cc[t] - cc[s])
                acc = acc + jnp.sum(p, axis=-1, keepdims=True) * vv[s]
            ho_ref[hh, pl.ds(t, nsub, stride=sub), :] = acc
        o_heads.append(o_inter[hh] + ho_ref[hh])
    og = og_ref[...]
    y_h = [o * lax.rsqrt(jnp.mean(o * o, axis=-1, keepdims=True) + NORM_EPS)
           * og[:, hh * HGRN_HEAD:(hh + 1) * HGRN_HEAD] for hh, o in enumerate(o_heads)]

    y = jnp.concatenate([y_pool] + y_h, axis=-1) * _silu(z)
    out = _dot(y, wout_ref[...])
    out_ref[0] = x + gate * _rms(out, postg_ref[...])


def _even_layer(res, ada, pre_g, post_g, w_in, w_out, pool_w, pool_scale, lb, onorm_g):
    bsz, t, d = res.shape
    tt = min(WIDE_ROW_TILE, t)
    pw = pool_scale.shape[-1]
    nh = pw // HGRN_HEAD
    row_spec = pl.BlockSpec((1, tt, d), lambda b, i: (b, i, 0))
    return pl.pallas_call(
        _even_kernel,
        grid=(bsz, t // tt),
        in_specs=[row_spec,
                  pl.BlockSpec((1, 1, 3 * d), lambda b, i: (b, 0, 0)),
                  _full_spec((1, d)), _full_spec((1, d)),
                  _full_spec(w_in.shape), _full_spec(w_out.shape), _full_spec(pool_w.shape),
                  _full_spec((1, pw)), _full_spec((1, pw)), _full_spec((1, pw))],
        out_specs=row_spec,
        out_shape=jax.ShapeDtypeStruct(res.shape, F32),
        scratch_shapes=[pltpu.VMEM((MAX_WINDOW + tt, pw), F32),
                        pltpu.VMEM((nh, HGRN_HEAD, HGRN_HEAD), F32)]
        + [pltpu.VMEM((nh, tt, HGRN_HEAD), F32) for _ in range(5)],
        compiler_params=_params(),
        name="even_layer",
    )(res, ada.reshape(bsz, 1, 3 * d), pre_g.reshape(1, d), post_g.reshape(1, d),
      w_in.astype(BF16), w_out.astype(BF16), pool_w.astype(BF16),
      pool_scale.reshape(1, pw), lb.reshape(1, pw), onorm_g.reshape(1, pw))


def _head_reduce_matrix(d):
    return jnp.where(_iota((d, HEAD_SLOTS), 0) // RWKV_HEAD == _iota((d, HEAD_SLOTS), 1), 1.0, 0.0).astype(BF16)


def _head_spread_matrix(d):
    return jnp.where(_iota((HEAD_SLOTS, d), 0) == _iota((HEAD_SLOTS, d), 1) // RWKV_HEAD, 1.0, 0.0).astype(BF16)


def _odd_pre_kernel(has_vfirst, *refs):
    if has_vfirst:
        (res_ref, ada_ref, preg_ref, mu_ref, w_ref, w0_ref, w1_ref, w2_ref, a0_ref, a1_ref, a2_ref,
         kk_ref, ka_ref, vf_ref, v0_ref, v1_ref, v2_ref,
         r_out, lw_out, k_out, v_out, kk_out, b_out, z_out, prev_ref) = refs
    else:
        (res_ref, ada_ref, preg_ref, mu_ref, w_ref, w0_ref, w1_ref, w2_ref, a0_ref, a1_ref, a2_ref,
         kk_ref, ka_ref,
         r_out, lw_out, k_out, v_out, kk_out, b_out, z_out, prev_ref) = refs
    ti = pl.program_id(1)
    tt, d = res_ref.shape[1], res_ref.shape[2]

    @pl.when(ti == 0)
    def _():
        prev_ref[...] = jnp.zeros(prev_ref.shape, F32)

    x = res_ref[0]
    ada = ada_ref[0]
    shift, scale = ada[:, :d], ada[:, d:2 * d]
    h = _rms(x, preg_ref[...]) * (1.0 + scale) + shift
    row = _iota((tt, 1), 0)
    hs = jnp.where(row == 0, prev_ref[0:1, :], pltpu.roll(h, 1, axis=0))
    prev_ref[0:1, :] = h[tt - 1:tt, :]
    xx = hs - h
    xm = lambda p: h + xx * mu_ref[p:p + 1, :]

    r = _dot(xm(0), w_ref[0])
    k = _dot(xm(1), w_ref[1])
    xv = xm(2)
    v = _dot(xv, w_ref[2])
    z_out[0] = _dot(xm(3), w_ref[3]).astype(z_out.dtype)
    logw = -_softplus(-(w0_ref[...] + _dot(jnp.tanh(_dot(xm(4), w1_ref[...])), w2_ref[...]))) - 0.5
    lw_out[0] = -jnp.exp(logw)
    a = _sigmoid(a0_ref[...] + _dot(_dot(xm(5), a1_ref[...]), a2_ref[...]))
    if has_vfirst:
        v = v + (vf_ref[0].astype(F32) - v) * _sigmoid(v0_ref[...] + _dot(_dot(xv, v1_ref[...]), v2_ref[...]))
    kkr = k * kk_ref[...]
    ss = _dot(kkr * kkr, _head_reduce_matrix(d))
    kk = kkr * _dot(1.0 / jnp.maximum(jnp.sqrt(ss), 1e-12), _head_spread_matrix(d))
    r_out[0] = r.astype(r_out.dtype)
    k_out[0] = (k * (1.0 + (a - 1.0) * ka_ref[...])).astype(k_out.dtype)
    v_out[0] = v.astype(v_out.dtype)
    kk_out[0] = kk.astype(kk_out.dtype)
    b_out[0] = (kk * a).astype(b_out.dtype)


def _odd_pre(res, ada, pre_g, mu, w_rkvz, w0, w1, w2, a0, a1, a2, k_k, k_a, vfirst=None):
    bsz, t, d = res.shape
    tt = min(WIDE_ROW_TILE, t)
    row_spec = pl.BlockSpec((1, tt, d), lambda b, i: (b, i, 0))
    vec = lambda a: a.reshape(1, d)
    args = [res, ada.reshape(bsz, 1, 3 * d), vec(pre_g), mu, w_rkvz.astype(BF16), vec(w0),
            w1.astype(BF16), w2.astype(BF16), vec(a0), a1.astype(BF16), a2.astype(BF16),
            vec(k_k), vec(k_a)]
    specs = [row_spec, pl.BlockSpec((1, 1, 3 * d), lambda b, i: (b, 0, 0)), _full_spec((1, d)),
             _full_spec(mu.shape), _full_spec(w_rkvz.shape, single_buffer=True), _full_spec((1, d)),
             _full_spec(w1.shape), _full_spec(w2.shape), _full_spec((1, d)),
             _full_spec(a1.shape), _full_spec(a2.shape), _full_spec((1, d)), _full_spec((1, d))]
    if vfirst is not None:
        v_first, v0, v1, v2 = vfirst
        args += [v_first, vec(v0), v1.astype(BF16), v2.astype(BF16)]
        specs += [row_spec, _full_spec((1, d)), _full_spec(v1.shape), _full_spec(v2.shape)]
    return pl.pallas_call(
        functools.partial(_odd_pre_kernel, vfirst is not None),
        grid=(bsz, t // tt),
        in_specs=specs,
        out_specs=[row_spec] * 7,
        out_shape=[jax.ShapeDtypeStruct(res.shape, dt) for dt in (BF16, F32, BF16, BF16, BF16, BF16, BF16)],
        scratch_shapes=[pltpu.VMEM((8, d), F32)],
        compiler_params=_params(),
        name="rwkv_pre",
    )(*args)


def _block_rows(y, n_heads, head):
    yb = y.astype(BF16)
    lane_head = _iota(yb.shape, 1) // head
    return jnp.concatenate([jnp.where(lane_head == hh, yb, jnp.zeros_like(yb)) for hh in range(n_heads)],
                           axis=0)


def _head_transpose(x, n_heads):
    w = x.shape[1] // n_heads
    return jnp.concatenate([x[:, hh * w:(hh + 1) * w] for hh in range(n_heads)], axis=0).T


def _scan_local_steps(ins, store):
    rs, lws, ks, vs, kks, bs = ins
    c_len, g = rs[0].shape
    nh = g // RWKV_HEAD
    hd = RWKV_HEAD
    each = lambda f, *ls: [f(*xs) for xs in zip(*ls)]
    bd = lambda y: _block_rows(y, nh, hd)
    mm = lambda x, y: _dot(x, bd(y))
    stack = lambda x, y: jnp.concatenate([x.astype(BF16), y.astype(BF16)], axis=0)
    wc = nh * c_len

    t_idx = _iota((c_len, wc), 0)
    s_idx = _iota((c_len, wc), 1) % c_len
    strict = s_idx < t_idx
    incl = s_idx <= t_idx
    diag_blk = strict & ((s_idx // RWKV_SUB) == (t_idx // RWKV_SUB))
    eye = jnp.where(s_idx == t_idx, 1.0, 0.0)
    dia = _iota((c_len, hd), 0) == _iota((c_len, hd), 1)

    e = {}
    steps = []

    def step(f):
        steps.append(f)
        return f

    @step
    def _():
        e["cs"] = each(lambda lw: _segment_cumsum(lw, c_len)[0], lws)

    @step
    def _():
        cs = e["cs"]
        cl = [c[c_len - 1:c_len, :] for c in cs]
        e["rt"] = each(lambda r, c: r * jnp.exp(c), rs, cs)
        e["kkt"] = each(lambda kk, c, lw: kk * jnp.exp(c - lw), kks, cs, lws)
        e["kh"] = each(lambda k, c: k * jnp.exp(-c), ks, cs)
        e["bh"] = each(lambda b, c: b * jnp.exp(-c), bs, cs)
        e["pend"] = [jnp.exp(l) for l in cl]
        e["kvec"] = each(lambda kh, p: kh * p, e["kh"], e["pend"])
        e["bvec"] = each(lambda bh, p: bh * p, e["bh"], e["pend"])

    @step
    def _():
        a_all = each(lambda kkt, rt, bh, kh: _dot_tb(stack(kkt, rt), stack(bd(bh), bd(kh))),
                     e["kkt"], e["rt"], e["bh"], e["kh"])
        a_kb = [a[:c_len, :wc] for a in a_all]
        e["a_kk"] = [jnp.where(strict, a[:c_len, wc:], 0.0) for a in a_all]
        e["a_rb"] = [jnp.where(incl, a[c_len:, :wc], 0.0) for a in a_all]
        e["a_rk"] = [jnp.where(incl, a[c_len:, wc:], 0.0) for a in a_all]
        e["n_p"] = each(lambda a: jnp.where(diag_blk, a, 0.0), a_kb)
        e["n_o"] = each(lambda a: jnp.where(strict & ~diag_blk, a, 0.0), a_kb)
        e["t_d"] = each(lambda n: eye - n, e["n_p"])

    @step
    def _():
        e["n_p"] = each(mm, e["n_p"], e["n_p"])

    for _ in range(RWKV_SUB.bit_length() - 3):
        @step
        def _():
            both = each(lambda t, n: _dot(stack(t, n), bd(n)), e["t_d"], e["n_p"])
            e["t_d"] = each(lambda t, s: t + s[:c_len], e["t_d"], both)
            e["n_p"] = [s[c_len:] for s in both]

    @step
    def _():
        e["t_d"] = each(lambda t, n: t + mm(t, n), e["t_d"], e["n_p"])

    @step
    def _():
        e["z_p"] = each(mm, e["t_d"], e["n_o"])
        e["t_m"] = each(lambda z: eye - z, e["z_p"])

    for _ in range((c_len // RWKV_SUB).bit_length() - 2):
        @step
        def _():
            e["z_p"] = each(mm, e["z_p"], e["z_p"])

        @step
        def _():
            e["t_m"] = each(lambda t, z: t + mm(t, z), e["t_m"], e["z_p"])

    @step
    def _():
        e["tmat"] = each(mm, e["t_m"], e["t_d"])

    @step
    def _():
        e["av"] = each(lambda akk, ark, kv, v: _dot(
            jnp.concatenate([akk.astype(BF16), ark.astype(BF16), _head_transpose(kv, nh).astype(BF16)], axis=0),
            bd(v)), e["a_kk"], e["a_rk"], e["kvec"], vs)

    @step
    def _():
        for i in range(len(rs)):
            pcat = jnp.concatenate(
                [jnp.broadcast_to(jnp.sum(jnp.where(dia, e["pend"][i][:, hh * hd:(hh + 1) * hd], 0.0),
                                          axis=1, keepdims=True), (c_len, hd)) for hh in range(nh)], axis=1)
            av = e["av"][i]
            store(i, kr=stack(e["kkt"][i], e["rt"][i]), tmat=e["tmat"][i].astype(BF16),
                  av=av[:c_len], arkv=av[c_len:2 * c_len], kv=av[2 * c_len:],
                  arbt=stack(e["a_rb"][i], _head_transpose(e["bvec"][i], nh)), pcat=pcat)

    return steps


def _scan_chain_steps(load, n_inst, states, emit):
    g = states[0].shape[1]
    nh = g // RWKV_HEAD
    c_len = RWKV_CHUNK
    bd = lambda y: _block_rows(y, nh, RWKV_HEAD)
    e = {}

    def s1():
        e["x"] = [_dot(load(i, "kr"), bd(states[i])) for i in range(n_inst)]

    def s2():
        e["u"] = [_dot(load(i, "tmat"), bd(e["x"][i][:c_len] + load(i, "av"))) for i in range(n_inst)]

    def s3():
        prod = [_dot(load(i, "arbt"), bd(e["u"][i])) for i in range(n_inst)]
        emit([e["x"][i][c_len:] + load(i, "arkv") - prod[i][:c_len] for i in range(n_inst)])
        for i in range(n_inst):
            states[i] = states[i] * load(i, "pcat") + load(i, "kv") - prod[i][c_len:]

    return [s1, s2, s3]


_SCAN_FIELDS = ("kr", "tmat", "av", "arkv", "arbt", "kv", "pcat")


def _scan_kernel(tiles_per_seq, r_ref, lw_ref, k_ref, v_ref, kk_ref, b_ref, y_ref, mt_ref, *bufs):
    ti = pl.program_id(1)
    tt, d = r_ref.shape[1], r_ref.shape[2]
    ng = d // GROUP_LANES
    nchunk = tt // RWKV_CHUNK
    ninst = nchunk * ng
    in_refs = (r_ref, lw_ref, k_ref, v_ref, kk_ref, b_ref)
    buf = dict(zip(_SCAN_FIELDS, bufs))
    wslot = (ti % 2) * ninst
    rslot = ((ti + 1) % 2) * ninst

    @pl.when(ti == 0)
    def _():
        mt_ref[...] = jnp.zeros(mt_ref.shape, F32)
        for name in _SCAN_FIELDS:
            ref = buf[name]
            ref[pl.ds(ninst, ninst)] = jnp.zeros((ninst,) + ref.shape[1:], ref.dtype)

    def store(i, **fields):
        for name, val in fields.items():
            buf[name][wslot + i] = val

    keep = jnp.where((ti + tiles_per_seq - 1) % tiles_per_seq == 0, 0.0, 1.0)
    states = [mt_ref[gi] * keep for gi in range(ng)]
    ys = []
    blocks, chain = [], []
    per_block = SCAN_CHUNKS_PER_STEP * ng
    for b0 in range(0, ninst, per_block):
        idx = range(b0, b0 + per_block)
        ins = [[ref[0, (i // ng) * RWKV_CHUNK:(i // ng + 1) * RWKV_CHUNK,
                    (i % ng) * GROUP_LANES:(i % ng + 1) * GROUP_LANES] for i in idx]
               for ref in in_refs]
        blocks.append(_scan_local_steps(ins, lambda i, _b0=b0, **f: store(_b0 + i, **f)))
    local = []
    for bi, steps in enumerate(blocks):
        at = min(len(local), bi * SCAN_BLOCK_STAGGER)
        merged = local[:at]
        rest = local[at:]
        for k in range(max(len(rest), len(steps))):
            merged += rest[k:k + 1] + steps[k:k + 1]
        local = merged
    for ci in range(nchunk):
        chain += _scan_chain_steps(lambda i, name, _c=ci: buf[name][rslot + _c * ng + i], ng,
                                   states, ys.append)
    done = 0
    span = max(1, (3 * len(local)) // 4)
    for j, f in enumerate(local):
        f()
        want = min(len(chain), -((j + 1) * len(chain) // -span))
        while done < want:
            chain[done]()
            done += 1
    for ci in range(nchunk):
        y_ref[0, ci * RWKV_CHUNK:(ci + 1) * RWKV_CHUNK, :] = jnp.concatenate(ys[ci], axis=-1)
    for gi in range(ng):
        mt_ref[gi] = states[gi]


def _rwkv_scan(r, lw, k, v, kk, b):
    bsz, t, d = r.shape
    tt = min(ROW_TILE, t)
    nt = bsz * (t // tt)
    ng = d // GROUP_LANES
    slots = 2 * (tt // RWKV_CHUNK) * ng
    c, g = RWKV_CHUNK, GROUP_LANES
    in_spec = pl.BlockSpec((1, tt, d), lambda bi, i: (bi, jnp.minimum(i, nt - 1), 0))
    out_spec = pl.BlockSpec((1, tt, d), lambda bi, i: (bi, jnp.maximum(i - 1, 0), 0))
    flat = lambda a: a.reshape(1, bsz * t, d)
    shapes = dict(kr=((2 * c, g), BF16), tmat=((c, g), BF16), av=((c, g), F32), arkv=((c, g), F32),
                  arbt=((c + RWKV_HEAD, g), BF16), kv=((RWKV_HEAD, g), F32), pcat=((RWKV_HEAD, g), F32))
    return pl.pallas_call(
        functools.partial(_scan_kernel, t // tt),
        grid=(1, nt + 1),
        in_specs=[in_spec] * 6,
        out_specs=out_spec,
        out_shape=jax.ShapeDtypeStruct((1, bsz * t, d), F32),
        scratch_shapes=[pltpu.VMEM((ng, RWKV_HEAD, g), F32)]
        + [pltpu.VMEM((slots,) + shapes[n][0], shapes[n][1]) for n in _SCAN_FIELDS],
        compiler_params=_params(),
        name="rwkv_scan",
    )(flat(r), flat(lw), flat(k), flat(v), flat(kk), flat(b)).reshape(bsz, t, d)


def _odd_post_kernel(res_ref, ada_ref, y_ref, r_ref, k_ref, v_ref, z_ref, rk_ref, lg_ref, lb_ref,
                     postg_ref, wout_ref, out_ref):
    d = res_ref.shape[2]
    tt = res_ref.shape[1]
    red, spread = _head_reduce_matrix(d), _head_spread_matrix(d)
    inv = 1.0 / RWKV_HEAD
    y = y_ref[0]
    rkr = r_ref[0].astype(F32) * k_ref[0].astype(F32) * rk_ref[...]
    sums = _dot(jnp.concatenate([y, rkr], axis=0), red)
    both = _dot(jnp.concatenate([sums[:tt] * inv, sums[tt:]], axis=0), spread)
    yc = y - both[:tt]
    rstd = lax.rsqrt(_dot(yc * yc, red) * inv + LNX_EPS)
    yn = yc * _dot(rstd, spread) * lg_ref[...] + lb_ref[...]
    out = _dot((yn + both[tt:] * v_ref[0].astype(F32)) * _silu(z_ref[0].astype(F32)), wout_ref[...])
    gate = ada_ref[0][:, 2 * d:]
    out_ref[0] = res_ref[0] + gate * _rms(out, postg_ref[...])


def _odd_post(res, ada, y, r, k, v, z, r_k, lnx_g, lnx_b, post_g, w_out):
    bsz, t, d = res.shape
    tt = min(WIDE_ROW_TILE, t)
    row_spec = pl.BlockSpec((1, tt, d), lambda b, i: (b, i, 0))
    vec = lambda a: a.reshape(1, d)
    return pl.pallas_call(
        _odd_post_kernel,
        grid=(bsz, t // tt),
        in_specs=[row_spec, pl.BlockSpec((1, 1, 3 * d), lambda b, i: (b, 0, 0))] + [row_spec] * 5
        + [_full_spec((1, d))] * 4 + [_full_spec(w_out.shape)],
        out_specs=row_spec,
        out_shape=jax.ShapeDtypeStruct(res.shape, F32),
        compiler_params=_params(),
        name="rwkv_post",
    )(res, ada.reshape(bsz, 1, 3 * d), y, r, k, v, z, vec(r_k), vec(lnx_g), vec(lnx_b), vec(post_g),
      w_out.astype(BF16))


def kernel(x, c, ada_w, ada_b, pre_g, post_g, ev_w_in, ev_w_out, pool_w, pool_scale, hgrn_lb_logits,
           hgrn_onorm_g, rw_mu, rw_w_rkvz, rw_w0, rw_w1, rw_w2, rw_a0, rw_a1, rw_a2, rw_k_k, rw_k_a,
           rw_r_k, rw_lnx_g, rw_lnx_b, rw_w_out, rw_v0, rw_v1, rw_v2):
    depth = ada_w.shape[0]
    res = x.astype(F32)
    ada = _ada_all(c, ada_w, ada_b)
    lb_all = _lower_bounds(hgrn_lb_logits)
    v_first = None
    for layer in range(depth):
        j = layer // 2
        if layer % 2 == 0:
            res = _even_layer(res, ada[layer], pre_g[layer], post_g[layer], ev_w_in[j], ev_w_out[j],
                              pool_w[j], pool_scale[j], lb_all[j], hgrn_onorm_g[j])
        else:
            vfirst = None if v_first is None else (v_first, rw_v0[j - 1], rw_v1[j - 1], rw_v2[j - 1])
            r, lw, k, v, kk, b, z = _odd_pre(res, ada[layer], pre_g[layer], rw_mu[j], rw_w_rkvz[j],
                                             rw_w0[j], rw_w1[j], rw_w2[j], rw_a0[j], rw_a1[j],
                                             rw_a2[j], rw_k_k[j], rw_k_a[j], vfirst)
            if v_first is None:
                v_first = v
            y = _rwkv_scan(r, lw, k, v, kk, b)
            res = _odd_post(res, ada[layer], y, r, k, v, z, rw_r_k[j], rw_lnx_g[j], rw_lnx_b[j],
                            post_g[layer], rw_w_out[j])
    return res.astype(x.dtype)
```

```python
import functools

import jax
import jax.numpy as jnp
from jax import lax
from jax.experimental import pallas as pl
from jax.experimental.pallas import tpu as pltpu

F32 = jnp.float32
BF16 = jnp.bfloat16

NORM_EPS = 1e-6
LOG2_E = 1.4426950408889634
LNX_EPS = 64e-5
POOL_WINDOWS = (2, 4, 8, 16)
MAX_WINDOW = 16
HGRN_HEAD = 128
HGRN_PITCH = 24
RWKV_HEAD = 64
RWKV_CHUNK = 64
RWKV_SUB = 16
HEAD_SLOTS = 128
GROUP_LANES = 256
SCAN_CHUNKS_PER_STEP = 2
SCAN_BLOCK_STAGGER = 5
ROW_TILE = 256
WIDE_ROW_TILE = 512
VMEM_LIMIT_BYTES = 60000 * 1024


def _dot(a, b):
    return jnp.dot(a.astype(BF16), b.astype(BF16), preferred_element_type=F32)


def _dot_tb(a, b):
    return lax.dot_general(a.astype(BF16), b.astype(BF16), (((1,), (1,)), ((), ())),
                           preferred_element_type=F32)


def _dot_ta(a, b):
    return lax.dot_general(a.astype(BF16), b.astype(BF16), (((0,), (0,)), ((), ())),
                           preferred_element_type=F32)


def _sigmoid(x):
    return jax.nn.sigmoid(x)


def _silu(x):
    return x * _sigmoid(x)


def _log1pexp_neg_abs(x):
    return jnp.log(1.0 + jnp.exp(-jnp.abs(x)))


def _softplus(x):
    return jnp.maximum(x, 0.0) + _log1pexp_neg_abs(x)


def _rms(x, g):
    return x * lax.rsqrt(jnp.mean(x * x, axis=-1, keepdims=True) + NORM_EPS) * g


def _iota(shape, dim):
    return lax.broadcasted_iota(jnp.int32, shape, dim)


def _segment_cumsum(x, seg):
    tile = 8
    sub = _iota(x.shape, 0) % tile
    sh = 1
    while sh < tile:
        x = x + jnp.where(sub >= sh, pltpu.roll(x, sh, axis=0), 0.0)
        sh *= 2
    per = seg // tile
    out, last, carry = [], [], None
    for j in range(x.shape[0] // tile):
        blk = x[j * tile:(j + 1) * tile, :]
        tot = blk[tile - 1:tile, :]
        if j % per:
            blk, tot = blk + carry, tot + carry
        out.append(blk)
        carry = tot
        if j % per == per - 1:
            last += [jnp.broadcast_to(tot, (seg, x.shape[1]))]
    return jnp.concatenate(out, axis=0), jnp.concatenate(last, axis=0)


def _full_spec(shape, single_buffer=False):
    nd = len(shape)
    mode = pl.Buffered(1) if single_buffer else None
    return pl.BlockSpec(shape, lambda *_: (0,) * nd, pipeline_mode=mode)


def _params():
    return pltpu.CompilerParams(dimension_semantics=("arbitrary", "arbitrary"),
                                vmem_limit_bytes=VMEM_LIMIT_BYTES)


def _ada_kernel(c_ref, w_ref, b_ref, o_ref):
    cond = _silu(c_ref[...].astype(F32))
    o_ref[0] = jnp.dot(cond, w_ref[0].astype(F32), preferred_element_type=F32,
                       precision=lax.Precision.HIGHEST) + b_ref[0]


def _ada_all(c, ada_w, ada_b):
    depth, d, d3 = ada_w.shape
    bsz = c.shape[0]
    tn = 1024
    return pl.pallas_call(
        _ada_kernel,
        grid=(depth, d3 // tn),
        in_specs=[pl.BlockSpec((bsz, d), lambda l, n: (0, 0)),
                  pl.BlockSpec((1, d, tn), lambda l, n: (l, 0, n)),
                  pl.BlockSpec((1, 1, tn), lambda l, n: (l, 0, n))],
        out_specs=pl.BlockSpec((1, bsz, tn), lambda l, n: (l, 0, n)),
        out_shape=jax.ShapeDtypeStruct((depth, bsz, d3), F32),
        compiler_params=_params(),
        name="ada_ln",
    )(c, ada_w, ada_b.reshape(depth, 1, d3))


def _lb_kernel(logit_ref, o_ref):
    x = logit_ref[...].astype(F32)
    n = x.shape[0]
    m = jnp.max(x, axis=0, keepdims=True)
    e = jnp.exp(x - m)
    sm = e / jnp.sum(e, axis=0, keepdims=True)
    acc = jnp.zeros_like(sm[0:1])
    for i in range(n):
        if i > 0:
            acc = acc + sm[i:i + 1]
        o_ref[i:i + 1, :] = acc


def _lower_bounds(logits):
    return pl.pallas_call(
        _lb_kernel,
        out_shape=jax.ShapeDtypeStruct(logits.shape, F32),
        name="hgrn_lower_bounds",
    )(logits)


def _even_kernel(res_ref, ada_ref, preg_ref, postg_ref, win_ref, wout_ref, poolw_ref, pscale_ref,
                 lb_ref, og_ref, out_ref,
                 pbuf, st_ref, hq_ref, hk_ref, hc_ref, hv_ref, ho_ref):
    ti = pl.program_id(1)
    tt, d = res_ref.shape[1], res_ref.shape[2]
    pw = pbuf.shape[1]
    nh = pw // HGRN_HEAD
    sub = MAX_WINDOW
    pad = MAX_WINDOW

    @pl.when(ti == 0)
    def _():
        pbuf[0:pad, :] = jnp.zeros((pad, pw), F32)
        st_ref[...] = jnp.zeros(st_ref.shape, F32)

    x = res_ref[0]
    ada = ada_ref[0]
    shift, scale, gate = ada[:, :d], ada[:, d:2 * d], ada[:, 2 * d:]
    h = _rms(x, preg_ref[...]) * (1.0 + scale) + shift
    u = _dot(h, win_ref[...])

    row = _iota((tt, 1), 0)
    up = u[:, :pw]
    pbuf[pad:pad + tt, :] = up
    pos1 = (ti * tt + row + 1).astype(F32)
    gw = pw // len(POOL_WINDOWS)
    pooled = []
    for gi, win in enumerate(POOL_WINDOWS):
        sl = slice(gi * gw, (gi + 1) * gw)
        s = pbuf[:, sl]
        sh = 1
        while sh < win:
            s = s + pltpu.roll(s, sh, axis=0)
            sh *= 2
        p = s[pad:, :] / jnp.minimum(pos1, float(win)) - up[:, sl]
        pooled.append(_dot(p, poolw_ref[gi]))
    pbuf[0:pad, :] = pbuf[tt:tt + pad, :]
    y_pool = jnp.concatenate(pooled, axis=-1) * pscale_ref[...]

    q = _silu(u[:, pw:2 * pw])
    fr = u[:, 2 * pw:3 * pw]
    v = u[:, 3 * pw:4 * pw]
    z = u[:, 4 * pw:]
    lb = lb_ref[...]
    k = (1.0 - lb) * _sigmoid(-fr)
    la = jnp.log(lb)
    lsig = jnp.minimum(fr, 0.0) - _log1pexp_neg_abs(fr)
    lbb = jnp.log1p(-lb) + lsig
    logf = jnp.maximum(la, lbb) + _log1pexp_neg_abs(la - lbb)

    cs, csl = _segment_cumsum(logf, sub)
    qt = q * jnp.exp(cs)
    kvec = k * jnp.exp(csl - cs)
    dl = jnp.exp(csl)
    nsub = tt // sub
    heads = [slice(hh * HGRN_HEAD, (hh + 1) * HGRN_HEAD) for hh in range(nh)]
    subs = [slice(m * sub, (m + 1) * sub) for m in range(nsub)]

    upd = [[_dot_ta(v[rows, hs], kvec[rows, hs]) for hs in heads] for rows in subs]
    start = []
    states = [st_ref[hh] for hh in range(nh)]
    for m, rows in enumerate(subs):
        start.append([s.astype(BF16) for s in states])
        states = [s * dl[m * sub:m * sub + 1, hs] + upd[m][hh]
                  for hh, (s, hs) in enumerate(zip(states, heads))]
    for hh in range(nh):
        st_ref[hh] = states[hh]
    o_inter = [jnp.concatenate([_dot_tb(qt[rows, hs], start[m][hh]) for m, rows in enumerate(subs)],
                               axis=0) for hh, hs in enumerate(heads)]

    cs2 = cs * LOG2_E
    for hh, hs in enumerate(heads):
        for m, rows in enumerate(subs):
            dst = slice(m * HGRN_PITCH, m * HGRN_PITCH + sub)
            hq_ref[hh, dst, :] = q[rows, hs]
            hk_ref[hh, dst, :] = k[rows, hs]
            hc_ref[hh, dst, :] = cs2[rows, hs]
            hv_ref[hh, dst, :] = v[rows, hs]
    o_heads = []
    for hh in range(nh):
        at = lambda ref, t: ref[hh, pl.ds(t, nsub, stride=HGRN_PITCH), :]
        qs = [at(hq_ref, t) for t in range(sub)]
        ks = [at(hk_ref, t) for t in range(sub)]
        cc = [at(hc_ref, t) for t in range(sub)]
        vv = [at(hv_ref, t) for t in range(sub)]
        for t in range(sub):
            acc = jnp.sum(qs[t] * ks[t], axis=-1, keepdims=True) * vv[t]
            for s in range(t):
                p = qs[t] * ks[s] * jnp.exp2(cc[t] - cc[s])
                acc = acc + jnp.sum(p, axis=-1, keepdims=True) * vv[s]
            ho_ref[hh, pl.ds(t, nsub, stride=HGRN_PITCH), :] = acc
        o_intra = jnp.concatenate([ho_ref[hh, m * HGRN_PITCH:m * HGRN_PITCH + sub, :] for m in range(nsub)],
                                  axis=0)
        o_heads.append(o_inter[hh] + o_intra)
    og = og_ref[...]
    y_h = [o * lax.rsqrt(jnp.mean(o * o, axis=-1, keepdims=True) + NORM_EPS)
           * og[:, hh * HGRN_HEAD:(hh + 1) * HGRN_HEAD] for hh, o in enumerate(o_heads)]

    y = jnp.concatenate([y_pool] + y_h, axis=-1) * _silu(z)
    out = _dot(y, wout_ref[...])
    out_ref[0] = x + gate * _rms(out, postg_ref[...])


def _even_layer(res, ada, pre_g, post_g, w_in, w_out, pool_w, pool_scale, lb, onorm_g):
    bsz, t, d = res.shape
    tt = min(WIDE_ROW_TILE, t)
    pw = pool_scale.shape[-1]
    nh = pw // HGRN_HEAD
    row_spec = pl.BlockSpec((1, tt, d), lambda b, i: (b, i, 0))
    return pl.pallas_call(
        _even_kernel,
        grid=(bsz, t // tt),
        in_specs=[row_spec,
                  pl.BlockSpec((1, 1, 3 * d), lambda b, i: (b, 0, 0)),
                  _full_spec((1, d)), _full_spec((1, d)),
                  _full_spec(w_in.shape), _full_spec(w_out.shape), _full_spec(pool_w.shape),
                  _full_spec((1, pw)), _full_spec((1, pw)), _full_spec((1, pw))],
        out_specs=row_spec,
        out_shape=jax.ShapeDtypeStruct(res.shape, F32),
        scratch_shapes=[pltpu.VMEM((MAX_WINDOW + tt, pw), F32),
                        pltpu.VMEM((nh, HGRN_HEAD, HGRN_HEAD), F32)]
        + [pltpu.VMEM((nh, (tt // MAX_WINDOW) * HGRN_PITCH, HGRN_HEAD), F32) for _ in range(5)],
        compiler_params=_params(),
        name="even_layer",
    )(res, ada.reshape(bsz, 1, 3 * d), pre_g.reshape(1, d), post_g.reshape(1, d),
      w_in.astype(BF16), w_out.astype(BF16), pool_w.astype(BF16),
      pool_scale.reshape(1, pw), lb.reshape(1, pw), onorm_g.reshape(1, pw))


def _head_reduce_matrix(d):
    return jnp.where(_iota((d, HEAD_SLOTS), 0) // RWKV_HEAD == _iota((d, HEAD_SLOTS), 1), 1.0, 0.0).astype(BF16)


def _head_spread_matrix(d):
    return jnp.where(_iota((HEAD_SLOTS, d), 0) == _iota((HEAD_SLOTS, d), 1) // RWKV_HEAD, 1.0, 0.0).astype(BF16)


def _odd_pre_kernel(has_vfirst, *refs):
    if has_vfirst:
        (res_ref, ada_ref, preg_ref, mu_ref, w_ref, w0_ref, w1_ref, w2_ref, a0_ref, a1_ref, a2_ref,
         kk_ref, ka_ref, vf_ref, v0_ref, v1_ref, v2_ref,
         r_out, lw_out, k_out, v_out, kk_out, b_out, z_out, prev_ref) = refs
    else:
        (res_ref, ada_ref, preg_ref, mu_ref, w_ref, w0_ref, w1_ref, w2_ref, a0_ref, a1_ref, a2_ref,
         kk_ref, ka_ref,
         r_out, lw_out, k_out, v_out, kk_out, b_out, z_out, prev_ref) = refs
    ti = pl.program_id(1)
    tt, d = res_ref.shape[1], res_ref.shape[2]

    @pl.when(ti == 0)
    def _():
        prev_ref[...] = jnp.zeros(prev_ref.shape, F32)

    x = res_ref[0]
    ada = ada_ref[0]
    shift, scale = ada[:, :d], ada[:, d:2 * d]
    h = _rms(x, preg_ref[...]) * (1.0 + scale) + shift
    row = _iota((tt, 1), 0)
    hs = jnp.where(row == 0, prev_ref[0:1, :], pltpu.roll(h, 1, axis=0))
    prev_ref[0:1, :] = h[tt - 1:tt, :]
    xx = hs - h
    xm = lambda p: h + xx * mu_ref[p:p + 1, :]

    r = _dot(xm(0), w_ref[0])
    k = _dot(xm(1), w_ref[1])
    xv = xm(2)
    v = _dot(xv, w_ref[2])
    z_out[0] = _dot(xm(3), w_ref[3])
    logw = -_softplus(-(w0_ref[...] + _dot(jnp.tanh(_dot(xm(4), w1_ref[...])), w2_ref[...]))) - 0.5
    lw_out[0] = -jnp.exp(logw)
    a = _sigmoid(a0_ref[...] + _dot(_dot(xm(5), a1_ref[...]), a2_ref[...]))
    if has_vfirst:
        v = v + (vf_ref[0] - v) * _sigmoid(v0_ref[...] + _dot(_dot(xv, v1_ref[...]), v2_ref[...]))
    kkr = k * kk_ref[...]
    ss = _dot(kkr * kkr, _head_reduce_matrix(d))
    kk = kkr * _dot(1.0 / jnp.maximum(jnp.sqrt(ss), 1e-12), _head_spread_matrix(d))
    r_out[0] = r
    k_out[0] = k * (1.0 + (a - 1.0) * ka_ref[...])
    v_out[0] = v
    kk_out[0] = kk
    b_out[0] = kk * a


def _odd_pre(res, ada, pre_g, mu, w_rkvz, w0, w1, w2, a0, a1, a2, k_k, k_a, vfirst=None):
    bsz, t, d = res.shape
    tt = min(WIDE_ROW_TILE, t)
    row_spec = pl.BlockSpec((1, tt, d), lambda b, i: (b, i, 0))
    vec = lambda a: a.reshape(1, d)
    args = [res, ada.reshape(bsz, 1, 3 * d), vec(pre_g), mu, w_rkvz.astype(BF16), vec(w0),
            w1.astype(BF16), w2.astype(BF16), vec(a0), a1.astype(BF16), a2.astype(BF16),
            vec(k_k), vec(k_a)]
    specs = [row_spec, pl.BlockSpec((1, 1, 3 * d), lambda b, i: (b, 0, 0)), _full_spec((1, d)),
             _full_spec(mu.shape), _full_spec(w_rkvz.shape, single_buffer=True), _full_spec((1, d)),
             _full_spec(w1.shape), _full_spec(w2.shape), _full_spec((1, d)),
             _full_spec(a1.shape), _full_spec(a2.shape), _full_spec((1, d)), _full_spec((1, d))]
    if vfirst is not None:
        v_first, v0, v1, v2 = vfirst
        args += [v_first, vec(v0), v1.astype(BF16), v2.astype(BF16)]
        specs += [row_spec, _full_spec((1, d)), _full_spec(v1.shape), _full_spec(v2.shape)]
    return pl.pallas_call(
        functools.partial(_odd_pre_kernel, vfirst is not None),
        grid=(bsz, t // tt),
        in_specs=specs,
        out_specs=[row_spec] * 7,
        out_shape=[jax.ShapeDtypeStruct(res.shape, F32)] * 7,
        scratch_shapes=[pltpu.VMEM((8, d), F32)],
        compiler_params=_params(),
        name="rwkv_pre",
    )(*args)


def _block_rows(y, n_heads, head):
    yb = y.astype(BF16)
    lane_head = _iota(yb.shape, 1) // head
    return jnp.concatenate([jnp.where(lane_head == hh, yb, jnp.zeros_like(yb)) for hh in range(n_heads)],
                           axis=0)


def _head_transpose(x, n_heads):
    w = x.shape[1] // n_heads
    return jnp.concatenate([x[:, hh * w:(hh + 1) * w] for hh in range(n_heads)], axis=0).T


def _scan_local_steps(ins, store):
    rs, lws, ks, vs, kks, bs = ins
    c_len, g = rs[0].shape
    nh = g // RWKV_HEAD
    hd = RWKV_HEAD
    each = lambda f, *ls: [f(*xs) for xs in zip(*ls)]
    bd = lambda y: _block_rows(y, nh, hd)
    mm = lambda x, y: _dot(x, bd(y))
    stack = lambda x, y: jnp.concatenate([x.astype(BF16), y.astype(BF16)], axis=0)
    wc = nh * c_len

    t_idx = _iota((c_len, wc), 0)
    s_idx = _iota((c_len, wc), 1) % c_len
    strict = s_idx < t_idx
    incl = s_idx <= t_idx
    diag_blk = strict & ((s_idx // RWKV_SUB) == (t_idx // RWKV_SUB))
    eye = jnp.where(s_idx == t_idx, 1.0, 0.0)
    dia = _iota((c_len, hd), 0) == _iota((c_len, hd), 1)

    e = {}
    steps = []

    def step(f):
        steps.append(f)
        return f

    @step
    def _():
        e["cs"] = each(lambda lw: _segment_cumsum(lw, c_len)[0], lws)

    @step
    def _():
        cs = e["cs"]
        cl = [c[c_len - 1:c_len, :] for c in cs]
        e["rt"] = each(lambda r, c: r * jnp.exp(c), rs, cs)
        e["kkt"] = each(lambda kk, c, lw: kk * jnp.exp(c - lw), kks, cs, lws)
        e["kh"] = each(lambda k, c: k * jnp.exp(-c), ks, cs)
        e["bh"] = each(lambda b, c: b * jnp.exp(-c), bs, cs)
        e["pend"] = [jnp.exp(l) for l in cl]
        e["kvec"] = each(lambda kh, p: kh * p, e["kh"], e["pend"])
        e["bvec"] = each(lambda bh, p: bh * p, e["bh"], e["pend"])

    @step
    def _():
        a_all = each(lambda kkt, rt, bh, kh: _dot_tb(stack(kkt, rt), stack(bd(bh), bd(kh))),
                     e["kkt"], e["rt"], e["bh"], e["kh"])
        a_kb = [a[:c_len, :wc] for a in a_all]
        e["a_kk"] = [jnp.where(strict, a[:c_len, wc:], 0.0) for a in a_all]
        e["a_rb"] = [jnp.where(incl, a[c_len:, :wc], 0.0) for a in a_all]
        e["a_rk"] = [jnp.where(incl, a[c_len:, wc:], 0.0) for a in a_all]
        e["n_p"] = each(lambda a: jnp.where(diag_blk, a, 0.0), a_kb)
        e["n_o"] = each(lambda a: jnp.where(strict & ~diag_blk, a, 0.0), a_kb)
        e["t_d"] = each(lambda n: eye - n, e["n_p"])

    @step
    def _():
        e["n_p"] = each(mm, e["n_p"], e["n_p"])

    for _ in range(RWKV_SUB.bit_length() - 3):
        @step
        def _():
            both = each(lambda t, n: _dot(stack(t, n), bd(n)), e["t_d"], e["n_p"])
            e["t_d"] = each(lambda t, s: t + s[:c_len], e["t_d"], both)
            e["n_p"] = [s[c_len:] for s in both]

    @step
    def _():
        e["t_d"] = each(lambda t, n: t + mm(t, n), e["t_d"], e["n_p"])

    @step
    def _():
        e["z_p"] = each(mm, e["t_d"], e["n_o"])
        e["t_m"] = each(lambda z: eye - z, e["z_p"])

    for _ in range((c_len // RWKV_SUB).bit_length() - 2):
        @step
        def _():
            e["z_p"] = each(mm, e["z_p"], e["z_p"])

        @step
        def _():
            e["t_m"] = each(lambda t, z: t + mm(t, z), e["t_m"], e["z_p"])

    @step
    def _():
        e["tmat"] = each(mm, e["t_m"], e["t_d"])

    @step
    def _():
        e["av"] = each(lambda akk, ark, kv, v: _dot(
            jnp.concatenate([akk.astype(BF16), ark.astype(BF16), _head_transpose(kv, nh).astype(BF16)], axis=0),
            bd(v)), e["a_kk"], e["a_rk"], e["kvec"], vs)

    @step
    def _():
        for i in range(len(rs)):
            pcat = jnp.concatenate(
                [jnp.broadcast_to(jnp.sum(jnp.where(dia, e["pend"][i][:, hh * hd:(hh + 1) * hd], 0.0),
                                          axis=1, keepdims=True), (c_len, hd)) for hh in range(nh)], axis=1)
            av = e["av"][i]
            store(i, kr=stack(e["kkt"][i], e["rt"][i]), tmat=e["tmat"][i].astype(BF16),
                  av=av[:c_len], arkv=av[c_len:2 * c_len], kv=av[2 * c_len:],
                  arbt=stack(e["a_rb"][i], _head_transpose(e["bvec"][i], nh)), pcat=pcat)

    return steps


def _scan_chain_steps(load, n_inst, states, emit):
    g = states[0].shape[1]
    nh = g // RWKV_HEAD
    c_len = RWKV_CHUNK
    bd = lambda y: _block_rows(y, nh, RWKV_HEAD)
    e = {}

    def s1():
        e["x"] = [_dot(load(i, "kr"), bd(states[i])) for i in range(n_inst)]

    def s2():
        e["u"] = [_dot(load(i, "tmat"), bd(e["x"][i][:c_len] + load(i, "av"))) for i in range(n_inst)]

    def s3():
        prod = [_dot(load(i, "arbt"), bd(e["u"][i])) for i in range(n_inst)]
        emit([e["x"][i][c_len:] + load(i, "arkv") - prod[i][:c_len] for i in range(n_inst)])
        for i in range(n_inst):
            states[i] = states[i] * load(i, "pcat") + load(i, "kv") - prod[i][c_len:]

    return [s1, s2, s3]


_SCAN_FIELDS = ("kr", "tmat", "av", "arkv", "arbt", "kv", "pcat")


def _scan_kernel(tiles_per_seq, r_ref, lw_ref, k_ref, v_ref, kk_ref, b_ref, y_ref, mt_ref, *bufs):
    ti = pl.program_id(1)
    tt, d = r_ref.shape[1], r_ref.shape[2]
    ng = d // GROUP_LANES
    nchunk = tt // RWKV_CHUNK
    ninst = nchunk * ng
    in_refs = (r_ref, lw_ref, k_ref, v_ref, kk_ref, b_ref)
    buf = dict(zip(_SCAN_FIELDS, bufs))
    wslot = (ti % 2) * ninst
    rslot = ((ti + 1) % 2) * ninst

    @pl.when(ti == 0)
    def _():
        mt_ref[...] = jnp.zeros(mt_ref.shape, F32)
        for name in _SCAN_FIELDS:
            ref = buf[name]
            ref[pl.ds(ninst, ninst)] = jnp.zeros((ninst,) + ref.shape[1:], ref.dtype)

    def store(i, **fields):
        for name, val in fields.items():
            buf[name][wslot + i] = val

    keep = jnp.where((ti + tiles_per_seq - 1) % tiles_per_seq == 0, 0.0, 1.0)
    states = [mt_ref[gi] * keep for gi in range(ng)]
    ys = []
    blocks, chain = [], []
    per_block = SCAN_CHUNKS_PER_STEP * ng
    for b0 in range(0, ninst, per_block):
        idx = range(b0, b0 + per_block)
        ins = [[ref[0, (i // ng) * RWKV_CHUNK:(i // ng + 1) * RWKV_CHUNK,
                    (i % ng) * GROUP_LANES:(i % ng + 1) * GROUP_LANES] for i in idx]
               for ref in in_refs]
        blocks.append(_scan_local_steps(ins, lambda i, _b0=b0, **f: store(_b0 + i, **f)))
    local = []
    for bi, steps in enumerate(blocks):
        at = min(len(local), bi * SCAN_BLOCK_STAGGER)
        merged = local[:at]
        rest = local[at:]
        for k in range(max(len(rest), len(steps))):
            merged += rest[k:k + 1] + steps[k:k + 1]
        local = merged
    for ci in range(nchunk):
        chain += _scan_chain_steps(lambda i, name, _c=ci: buf[name][rslot + _c * ng + i], ng,
                                   states, ys.append)
    done = 0
    span = max(1, (3 * len(local)) // 4)
    for j, f in enumerate(local):
        f()
        want = min(len(chain), -((j + 1) * len(chain) // -span))
        while done < want:
            chain[done]()
            done += 1
    for ci in range(nchunk):
        y_ref[0, ci * RWKV_CHUNK:(ci + 1) * RWKV_CHUNK, :] = jnp.concatenate(ys[ci], axis=-1)
    for gi in range(ng):
        mt_ref[gi] = states[gi]


def _rwkv_scan(r, lw, k, v, kk, b):
    bsz, t, d = r.shape
    tt = min(ROW_TILE, t)
    nt = bsz * (t // tt)
    ng = d // GROUP_LANES
    slots = 2 * (tt // RWKV_CHUNK) * ng
    c, g = RWKV_CHUNK, GROUP_LANES
    in_spec = pl.BlockSpec((1, tt, d), lambda bi, i: (bi, jnp.minimum(i, nt - 1), 0))
    out_spec = pl.BlockSpec((1, tt, d), lambda bi, i: (bi, jnp.maximum(i - 1, 0), 0))
    flat = lambda a: a.reshape(1, bsz * t, d)
    shapes = dict(kr=((2 * c, g), BF16), tmat=((c, g), BF16), av=((c, g), F32), arkv=((c, g), F32),
                  arbt=((c + RWKV_HEAD, g), BF16), kv=((RWKV_HEAD, g), F32), pcat=((RWKV_HEAD, g), F32))
    return pl.pallas_call(
        functools.partial(_scan_kernel, t // tt),
        grid=(1, nt + 1),
        in_specs=[in_spec] * 6,
        out_specs=out_spec,
        out_shape=jax.ShapeDtypeStruct((1, bsz * t, d), F32),
        scratch_shapes=[pltpu.VMEM((ng, RWKV_HEAD, g), F32)]
        + [pltpu.VMEM((slots,) + shapes[n][0], shapes[n][1]) for n in _SCAN_FIELDS],
        compiler_params=_params(),
        name="rwkv_scan",
    )(flat(r), flat(lw), flat(k), flat(v), flat(kk), flat(b)).reshape(bsz, t, d)


def _odd_post_kernel(res_ref, ada_ref, y_ref, r_ref, k_ref, v_ref, z_ref, rk_ref, lg_ref, lb_ref,
                     postg_ref, wout_ref, out_ref):
    d = res_ref.shape[2]
    tt = res_ref.shape[1]
    red, spread = _head_reduce_matrix(d), _head_spread_matrix(d)
    inv = 1.0 / RWKV_HEAD
    y = y_ref[0]
    rkr = r_ref[0] * k_ref[0] * rk_ref[...]
    sums = _dot(jnp.concatenate([y, rkr], axis=0), red)
    both = _dot(jnp.concatenate([sums[:tt] * inv, sums[tt:]], axis=0), spread)
    yc = y - both[:tt]
    rstd = lax.rsqrt(_dot(yc * yc, red) * inv + LNX_EPS)
    yn = yc * _dot(rstd, spread) * lg_ref[...] + lb_ref[...]
    out = _dot((yn + both[tt:] * v_ref[0]) * _silu(z_ref[0]), wout_ref[...])
    gate = ada_ref[0][:, 2 * d:]
    out_ref[0] = res_ref[0] + gate * _rms(out, postg_ref[...])


def _odd_post(res, ada, y, r, k, v, z, r_k, lnx_g, lnx_b, post_g, w_out):
    bsz, t, d = res.shape
    tt = min(WIDE_ROW_TILE, t)
    row_spec = pl.BlockSpec((1, tt, d), lambda b, i: (b, i, 0))
    vec = lambda a: a.reshape(1, d)
    return pl.pallas_call(
        _odd_post_kernel,
        grid=(bsz, t // tt),
        in_specs=[row_spec, pl.BlockSpec((1, 1, 3 * d), lambda b, i: (b, 0, 0))] + [row_spec] * 5
        + [_full_spec((1, d))] * 4 + [_full_spec(w_out.shape)],
        out_specs=row_spec,
        out_shape=jax.ShapeDtypeStruct(res.shape, F32),
        compiler_params=_params(),
        name="rwkv_post",
    )(res, ada.reshape(bsz, 1, 3 * d), y, r, k, v, z, vec(r_k), vec(lnx_g), vec(lnx_b), vec(post_g),
      w_out.astype(BF16))


def kernel(x, c, ada_w, ada_b, pre_g, post_g, ev_w_in, ev_w_out, pool_w, pool_scale, hgrn_lb_logits,
           hgrn_onorm_g, rw_mu, rw_w_rkvz, rw_w0, rw_w1, rw_w2, rw_a0, rw_a1, rw_a2, rw_k_k, rw_k_a,
           rw_r_k, rw_lnx_g, rw_lnx_b, rw_w_out, rw_v0, rw_v1, rw_v2):
    depth = ada_w.shape[0]
    res = x.astype(F32)
    ada = _ada_all(c, ada_w, ada_b)
    lb_all = _lower_bounds(hgrn_lb_logits)
    v_first = None
    for layer in range(depth):
        j = layer // 2
        if layer % 2 == 0:
            res = _even_layer(res, ada[layer], pre_g[layer], post_g[layer], ev_w_in[j], ev_w_out[j],
                              pool_w[j], pool_scale[j], lb_all[j], hgrn_onorm_g[j])
        else:
            vfirst = None if v_first is None else (v_first, rw_v0[j - 1], rw_v1[j - 1], rw_v2[j - 1])
            r, lw, k, v, kk, b, z = _odd_pre(res, ada[layer], pre_g[layer], rw_mu[j], rw_w_rkvz[j],
                                             rw_w0[j], rw_w1[j], rw_w2[j], rw_a0[j], rw_a1[j],
                                             rw_a2[j], rw_k_k[j], rw_k_a[j], vfirst)
            if v_first is None:
                v_first = v
            y = _rwkv_scan(r, lw, k, v, kk, b)
            res = _odd_post(res, ada[layer], y, r, k, v, z, rw_r_k[j], rw_lnx_g[j], rw_lnx_b[j],
                            post_g[layer], rw_w_out[j])
    return res.astype(x.dtype)
```

```python
import functools

import jax
import jax.numpy as jnp
from jax import lax
from jax.experimental import pallas as pl
from jax.experimental.pallas import tpu as pltpu

F32 = jnp.float32
BF16 = jnp.bfloat16

NORM_EPS = 1e-6
LOG2_E = 1.4426950408889634
LNX_EPS = 64e-5
POOL_WINDOWS = (2, 4, 8, 16)
MAX_WINDOW = 16
HGRN_HEAD = 128
HGRN_PITCH = 24
RWKV_HEAD = 64
RWKV_CHUNK = 64
RWKV_SUB = 8
HEAD_SLOTS = 128
GROUP_LANES = 256
SCAN_CHUNKS_PER_STEP = 2
SCAN_BLOCK_STAGGER = 5
ROW_TILE = 256
WIDE_ROW_TILE = 512
VMEM_LIMIT_BYTES = 60000 * 1024


def _dot(a, b):
    return jnp.dot(a.astype(BF16), b.astype(BF16), preferred_element_type=F32)


def _dot_tb(a, b):
    return lax.dot_general(a.astype(BF16), b.astype(BF16), (((1,), (1,)), ((), ())),
                           preferred_element_type=F32)


def _dot_ta(a, b):
    return lax.dot_general(a.astype(BF16), b.astype(BF16), (((0,), (0,)), ((), ())),
                           preferred_element_type=F32)


def _sigmoid(x):
    return jax.nn.sigmoid(x)


def _silu(x):
    return x * _sigmoid(x)


def _log1pexp_neg_abs(x):
    return jnp.log(1.0 + jnp.exp(-jnp.abs(x)))


def _softplus(x):
    return jnp.maximum(x, 0.0) + _log1pexp_neg_abs(x)


def _rms(x, g):
    return x * lax.rsqrt(jnp.mean(x * x, axis=-1, keepdims=True) + NORM_EPS) * g


def _iota(shape, dim):
    return lax.broadcasted_iota(jnp.int32, shape, dim)


def _segment_cumsum(x, seg):
    tile = 8
    sub = _iota(x.shape, 0) % tile
    sh = 1
    while sh < tile:
        x = x + jnp.where(sub >= sh, pltpu.roll(x, sh, axis=0), 0.0)
        sh *= 2
    per = seg // tile
    out, last, carry = [], [], None
    for j in range(x.shape[0] // tile):
        blk = x[j * tile:(j + 1) * tile, :]
        tot = blk[tile - 1:tile, :]
        if j % per:
            blk, tot = blk + carry, tot + carry
        out.append(blk)
        carry = tot
        if j % per == per - 1:
            last += [jnp.broadcast_to(tot, (seg, x.shape[1]))]
    return jnp.concatenate(out, axis=0), jnp.concatenate(last, axis=0)


def _full_spec(shape, single_buffer=False):
    nd = len(shape)
    mode = pl.Buffered(1) if single_buffer else None
    return pl.BlockSpec(shape, lambda *_: (0,) * nd, pipeline_mode=mode)


def _params():
    return pltpu.CompilerParams(dimension_semantics=("arbitrary", "arbitrary"),
                                vmem_limit_bytes=VMEM_LIMIT_BYTES)


def _ada_kernel(c_ref, w_ref, b_ref, o_ref):
    cond = _silu(c_ref[...].astype(F32))
    o_ref[0] = jnp.dot(cond, w_ref[0].astype(F32), preferred_element_type=F32,
                       precision=lax.Precision.HIGHEST) + b_ref[0]


def _ada_all(c, ada_w, ada_b):
    depth, d, d3 = ada_w.shape
    bsz = c.shape[0]
    tn = 1024
    return pl.pallas_call(
        _ada_kernel,
        grid=(depth, d3 // tn),
        in_specs=[pl.BlockSpec((bsz, d), lambda l, n: (0, 0)),
                  pl.BlockSpec((1, d, tn), lambda l, n: (l, 0, n)),
                  pl.BlockSpec((1, 1, tn), lambda l, n: (l, 0, n))],
        out_specs=pl.BlockSpec((1, bsz, tn), lambda l, n: (l, 0, n)),
        out_shape=jax.ShapeDtypeStruct((depth, bsz, d3), F32),
        compiler_params=_params(),
        name="ada_ln",
    )(c, ada_w, ada_b.reshape(depth, 1, d3))


def _lb_kernel(logit_ref, o_ref):
    x = logit_ref[...].astype(F32)
    n = x.shape[0]
    m = jnp.max(x, axis=0, keepdims=True)
    e = jnp.exp(x - m)
    sm = e / jnp.sum(e, axis=0, keepdims=True)
    acc = jnp.zeros_like(sm[0:1])
    for i in range(n):
        if i > 0:
            acc = acc + sm[i:i + 1]
        o_ref[i:i + 1, :] = acc


def _lower_bounds(logits):
    return pl.pallas_call(
        _lb_kernel,
        out_shape=jax.ShapeDtypeStruct(logits.shape, F32),
        name="hgrn_lower_bounds",
    )(logits)


def _even_kernel(res_ref, ada_ref, preg_ref, postg_ref, win_ref, wout_ref, poolw_ref, pscale_ref,
                 lb_ref, og_ref, out_ref,
                 pbuf, st_ref, hq_ref, hk_ref, hc_ref, hv_ref, ho_ref):
    ti = pl.program_id(1)
    tt, d = res_ref.shape[1], res_ref.shape[2]
    pw = pbuf.shape[1]
    nh = pw // HGRN_HEAD
    sub = MAX_WINDOW
    pad = MAX_WINDOW

    @pl.when(ti == 0)
    def _():
        pbuf[0:pad, :] = jnp.zeros((pad, pw), F32)
        st_ref[...] = jnp.zeros(st_ref.shape, F32)

    x = res_ref[0]
    ada = ada_ref[0]
    shift, scale, gate = ada[:, :d], ada[:, d:2 * d], ada[:, 2 * d:]
    h = _rms(x, preg_ref[...]) * (1.0 + scale) + shift
    u = _dot(h, win_ref[...])

    row = _iota((tt, 1), 0)
    up = u[:, :pw]
    pbuf[pad:pad + tt, :] = up
    pos1 = (ti * tt + row + 1).astype(F32)
    gw = pw // len(POOL_WINDOWS)
    pooled = []
    for gi, win in enumerate(POOL_WINDOWS):
        sl = slice(gi * gw, (gi + 1) * gw)
        s = pbuf[:, sl]
        sh = 1
        while sh < win:
            s = s + pltpu.roll(s, sh, axis=0)
            sh *= 2
        p = s[pad:, :] / jnp.minimum(pos1, float(win)) - up[:, sl]
        pooled.append(_dot(p, poolw_ref[gi]))
    pbuf[0:pad, :] = pbuf[tt:tt + pad, :]
    y_pool = jnp.concatenate(pooled, axis=-1) * pscale_ref[...]

    q = _silu(u[:, pw:2 * pw])
    fr = u[:, 2 * pw:3 * pw]
    v = u[:, 3 * pw:4 * pw]
    z = u[:, 4 * pw:]
    lb = lb_ref[...]
    k = (1.0 - lb) * _sigmoid(-fr)
    la = jnp.log(lb)
    lsig = jnp.minimum(fr, 0.0) - _log1pexp_neg_abs(fr)
    lbb = jnp.log1p(-lb) + lsig
    logf = jnp.maximum(la, lbb) + _log1pexp_neg_abs(la - lbb)

    cs, csl = _segment_cumsum(logf, sub)
    qt = q * jnp.exp(cs)
    kvec = k * jnp.exp(csl - cs)
    dl = jnp.exp(csl)
    nsub = tt // sub
    heads = [slice(hh * HGRN_HEAD, (hh + 1) * HGRN_HEAD) for hh in range(nh)]
    subs = [slice(m * sub, (m + 1) * sub) for m in range(nsub)]

    upd = [[_dot_ta(v[rows, hs], kvec[rows, hs]) for hs in heads] for rows in subs]
    start = []
    states = [st_ref[hh] for hh in range(nh)]
    for m, rows in enumerate(subs):
        start.append([s.astype(BF16) for s in states])
        states = [s * dl[m * sub:m * sub + 1, hs] + upd[m][hh]
                  for hh, (s, hs) in enumerate(zip(states, heads))]
    for hh in range(nh):
        st_ref[hh] = states[hh]
    o_inter = [jnp.concatenate([_dot_tb(qt[rows, hs], start[m][hh]) for m, rows in enumerate(subs)],
                               axis=0) for hh, hs in enumerate(heads)]

    cs2 = cs * LOG2_E
    for hh, hs in enumerate(heads):
        for m, rows in enumerate(subs):
            dst = slice(m * HGRN_PITCH, m * HGRN_PITCH + sub)
            hq_ref[hh, dst, :] = q[rows, hs]
            hk_ref[hh, dst, :] = k[rows, hs]
            hc_ref[hh, dst, :] = cs2[rows, hs]
            hv_ref[hh, dst, :] = v[rows, hs]
    o_heads = []
    for hh in range(nh):
        at = lambda ref, t: ref[hh, pl.ds(t, nsub, stride=HGRN_PITCH), :]
        qs = [at(hq_ref, t) for t in range(sub)]
        ks = [at(hk_ref, t) for t in range(sub)]
        cc = [at(hc_ref, t) for t in range(sub)]
        vv = [at(hv_ref, t) for t in range(sub)]
        for t in range(sub):
            acc = jnp.sum(qs[t] * ks[t], axis=-1, keepdims=True) * vv[t]
            for s in range(t):
                p = qs[t] * ks[s] * jnp.exp2(cc[t] - cc[s])
                acc = acc + jnp.sum(p, axis=-1, keepdims=True) * vv[s]
            ho_ref[hh, pl.ds(t, nsub, stride=HGRN_PITCH), :] = acc
        o_intra = jnp.concatenate([ho_ref[hh, m * HGRN_PITCH:m * HGRN_PITCH + sub, :] for m in range(nsub)],
                                  axis=0)
        o_heads.append(o_inter[hh] + o_intra)
    og = og_ref[...]
    y_h = [o * lax.rsqrt(jnp.mean(o * o, axis=-1, keepdims=True) + NORM_EPS)
           * og[:, hh * HGRN_HEAD:(hh + 1) * HGRN_HEAD] for hh, o in enumerate(o_heads)]

    y = jnp.concatenate([y_pool] + y_h, axis=-1) * _silu(z)
    out = _dot(y, wout_ref[...])
    out_ref[0] = x + gate * _rms(out, postg_ref[...])


def _even_layer(res, ada, pre_g, post_g, w_in, w_out, pool_w, pool_scale, lb, onorm_g):
    bsz, t, d = res.shape
    tt = min(WIDE_ROW_TILE, t)
    pw = pool_scale.shape[-1]
    nh = pw // HGRN_HEAD
    row_spec = pl.BlockSpec((1, tt, d), lambda b, i: (b, i, 0))
    return pl.pallas_call(
        _even_kernel,
        grid=(bsz, t // tt),
        in_specs=[row_spec,
                  pl.BlockSpec((1, 1, 3 * d), lambda b, i: (b, 0, 0)),
                  _full_spec((1, d)), _full_spec((1, d)),
                  _full_spec(w_in.shape), _full_spec(w_out.shape), _full_spec(pool_w.shape),
                  _full_spec((1, pw)), _full_spec((1, pw)), _full_spec((1, pw))],
        out_specs=row_spec,
        out_shape=jax.ShapeDtypeStruct(res.shape, F32),
        scratch_shapes=[pltpu.VMEM((MAX_WINDOW + tt, pw), F32),
                        pltpu.VMEM((nh, HGRN_HEAD, HGRN_HEAD), F32)]
        + [pltpu.VMEM((nh, (tt // MAX_WINDOW) * HGRN_PITCH, HGRN_HEAD), F32) for _ in range(5)],
        compiler_params=_params(),
        name="even_layer",
    )(res, ada.reshape(bsz, 1, 3 * d), pre_g.reshape(1, d), post_g.reshape(1, d),
      w_in.astype(BF16), w_out.astype(BF16), pool_w.astype(BF16),
      pool_scale.reshape(1, pw), lb.reshape(1, pw), onorm_g.reshape(1, pw))


def _head_reduce_matrix(d):
    return jnp.where(_iota((d, HEAD_SLOTS), 0) // RWKV_HEAD == _iota((d, HEAD_SLOTS), 1), 1.0, 0.0).astype(BF16)


def _head_spread_matrix(d):
    return jnp.where(_iota((HEAD_SLOTS, d), 0) == _iota((HEAD_SLOTS, d), 1) // RWKV_HEAD, 1.0, 0.0).astype(BF16)


def _odd_pre_kernel(has_vfirst, *refs):
    if has_vfirst:
        (res_ref, ada_ref, preg_ref, mu_ref, w_ref, w0_ref, w1_ref, w2_ref, a0_ref, a1_ref, a2_ref,
         kk_ref, ka_ref, vf_ref, v0_ref, v1_ref, v2_ref,
         r_out, lw_out, k_out, v_out, kk_out, b_out, z_out, prev_ref) = refs
    else:
        (res_ref, ada_ref, preg_ref, mu_ref, w_ref, w0_ref, w1_ref, w2_ref, a0_ref, a1_ref, a2_ref,
         kk_ref, ka_ref,
         r_out, lw_out, k_out, v_out, kk_out, b_out, z_out, prev_ref) = refs
    ti = pl.program_id(1)
    tt, d = res_ref.shape[1], res_ref.shape[2]

    @pl.when(ti == 0)
    def _():
        prev_ref[...] = jnp.zeros(prev_ref.shape, F32)

    x = res_ref[0]
    ada = ada_ref[0]
    shift, scale = ada[:, :d], ada[:, d:2 * d]
    h = _rms(x, preg_ref[...]) * (1.0 + scale) + shift
    row = _iota((tt, 1), 0)
    hs = jnp.where(row == 0, prev_ref[0:1, :], pltpu.roll(h, 1, axis=0))
    prev_ref[0:1, :] = h[tt - 1:tt, :]
    xx = hs - h
    xm = lambda p: h + xx * mu_ref[p:p + 1, :]

    r = _dot(xm(0), w_ref[0])
    k = _dot(xm(1), w_ref[1])
    xv = xm(2)
    v = _dot(xv, w_ref[2])
    z_out[0] = _dot(xm(3), w_ref[3])
    logw = -_softplus(-(w0_ref[...] + _dot(jnp.tanh(_dot(xm(4), w1_ref[...])), w2_ref[...]))) - 0.5
    lw_out[0] = -jnp.exp(logw)
    a = _sigmoid(a0_ref[...] + _dot(_dot(xm(5), a1_ref[...]), a2_ref[...]))
    if has_vfirst:
        v = v + (vf_ref[0] - v) * _sigmoid(v0_ref[...] + _dot(_dot(xv, v1_ref[...]), v2_ref[...]))
    kkr = k * kk_ref[...]
    ss = _dot(kkr * kkr, _head_reduce_matrix(d))
    kk = kkr * _dot(1.0 / jnp.maximum(jnp.sqrt(ss), 1e-12), _head_spread_matrix(d))
    r_out[0] = r
    k_out[0] = k * (1.0 + (a - 1.0) * ka_ref[...])
    v_out[0] = v
    kk_out[0] = kk
    b_out[0] = kk * a


def _odd_pre(res, ada, pre_g, mu, w_rkvz, w0, w1, w2, a0, a1, a2, k_k, k_a, vfirst=None):
    bsz, t, d = res.shape
    tt = min(WIDE_ROW_TILE, t)
    row_spec = pl.BlockSpec((1, tt, d), lambda b, i: (b, i, 0))
    vec = lambda a: a.reshape(1, d)
    args = [res, ada.reshape(bsz, 1, 3 * d), vec(pre_g), mu, w_rkvz.astype(BF16), vec(w0),
            w1.astype(BF16), w2.astype(BF16), vec(a0), a1.astype(BF16), a2.astype(BF16),
            vec(k_k), vec(k_a)]
    specs = [row_spec, pl.BlockSpec((1, 1, 3 * d), lambda b, i: (b, 0, 0)), _full_spec((1, d)),
             _full_spec(mu.shape), _full_spec(w_rkvz.shape, single_buffer=True), _full_spec((1, d)),
             _full_spec(w1.shape), _full_spec(w2.shape), _full_spec((1, d)),
             _full_spec(a1.shape), _full_spec(a2.shape), _full_spec((1, d)), _full_spec((1, d))]
    if vfirst is not None:
        v_first, v0, v1, v2 = vfirst
        args += [v_first, vec(v0), v1.astype(BF16), v2.astype(BF16)]
        specs += [row_spec, _full_spec((1, d)), _full_spec(v1.shape), _full_spec(v2.shape)]
    return pl.pallas_call(
        functools.partial(_odd_pre_kernel, vfirst is not None),
        grid=(bsz, t // tt),
        in_specs=specs,
        out_specs=[row_spec] * 7,
        out_shape=[jax.ShapeDtypeStruct(res.shape, F32)] * 7,
        scratch_shapes=[pltpu.VMEM((8, d), F32)],
        compiler_params=_params(),
        name="rwkv_pre",
    )(*args)


def _block_rows(y, n_heads, head):
    yb = y.astype(BF16)
    lane_head = _iota(yb.shape, 1) // head
    return jnp.concatenate([jnp.where(lane_head == hh, yb, jnp.zeros_like(yb)) for hh in range(n_heads)],
                           axis=0)


def _head_transpose(x, n_heads):
    w = x.shape[1] // n_heads
    return jnp.concatenate([x[:, hh * w:(hh + 1) * w] for hh in range(n_heads)], axis=0).T


def _scan_local_steps(ins, store):
    rs, lws, ks, vs, kks, bs = ins
    c_len, g = rs[0].shape
    nh = g // RWKV_HEAD
    hd = RWKV_HEAD
    each = lambda f, *ls: [f(*xs) for xs in zip(*ls)]
    bd = lambda y: _block_rows(y, nh, hd)
    mm = lambda x, y: _dot(x, bd(y))
    stack = lambda x, y: jnp.concatenate([x.astype(BF16), y.astype(BF16)], axis=0)
    wc = nh * c_len

    t_idx = _iota((c_len, wc), 0)
    s_idx = _iota((c_len, wc), 1) % c_len
    strict = s_idx < t_idx
    incl = s_idx <= t_idx
    diag_blk = strict & ((s_idx // RWKV_SUB) == (t_idx // RWKV_SUB))
    eye = jnp.where(s_idx == t_idx, 1.0, 0.0)
    dia = _iota((c_len, hd), 0) == _iota((c_len, hd), 1)

    e = {}
    steps = []

    def step(f):
        steps.append(f)
        return f

    @step
    def _():
        e["cs"] = each(lambda lw: _segment_cumsum(lw, c_len)[0], lws)

    @step
    def _():
        cs = e["cs"]
        cl = [c[c_len - 1:c_len, :] for c in cs]
        e["rt"] = each(lambda r, c: r * jnp.exp(c), rs, cs)
        e["kkt"] = each(lambda kk, c, lw: kk * jnp.exp(c - lw), kks, cs, lws)
        e["kh"] = each(lambda k, c: k * jnp.exp(-c), ks, cs)
        e["bh"] = each(lambda b, c: b * jnp.exp(-c), bs, cs)
        e["pend"] = [jnp.exp(l) for l in cl]
        e["kvec"] = each(lambda kh, p: kh * p, e["kh"], e["pend"])
        e["bvec"] = each(lambda bh, p: bh * p, e["bh"], e["pend"])

    @step
    def _():
        a_all = each(lambda kkt, rt, bh, kh: _dot_tb(stack(kkt, rt), stack(bd(bh), bd(kh))),
                     e["kkt"], e["rt"], e["bh"], e["kh"])
        a_kb = [a[:c_len, :wc] for a in a_all]
        e["a_kk"] = [jnp.where(strict, a[:c_len, wc:], 0.0) for a in a_all]
        e["a_rb"] = [jnp.where(incl, a[c_len:, :wc], 0.0) for a in a_all]
        e["a_rk"] = [jnp.where(incl, a[c_len:, wc:], 0.0) for a in a_all]
        e["p"] = each(lambda a: jnp.where(diag_blk, a, 0.0), a_kb)
        e["n_o"] = each(lambda a: jnp.where(strict & ~diag_blk, a, 0.0), a_kb)

    def nilpotent_inverse(index, out):
        @step
        def _():
            e[out] = each(lambda p: eye - p, e["p"])
            e["p"] = each(mm, e["p"], e["p"])

        for _ in range(index.bit_length() - 3):
            @step
            def _():
                both = each(lambda t, p: _dot(stack(t, p), bd(p)), e[out], e["p"])
                e[out] = each(lambda t, s: t + s[:c_len], e[out], both)
                e["p"] = [s[c_len:] for s in both]

        @step
        def _():
            e[out] = each(lambda t, p: t + mm(t, p), e[out], e["p"])

    nilpotent_inverse(RWKV_SUB, "t_d")

    @step
    def _():
        e["p"] = each(mm, e["t_d"], e["n_o"])

    nilpotent_inverse(c_len // RWKV_SUB, "t_m")

    @step
    def _():
        e["tmat"] = each(mm, e["t_m"], e["t_d"])

    @step
    def _():
        e["av"] = each(lambda akk, ark, kv, v: _dot(
            jnp.concatenate([akk.astype(BF16), ark.astype(BF16), _head_transpose(kv, nh).astype(BF16)], axis=0),
            bd(v)), e["a_kk"], e["a_rk"], e["kvec"], vs)

    @step
    def _():
        for i in range(len(rs)):
            pcat = jnp.concatenate(
                [jnp.broadcast_to(jnp.sum(jnp.where(dia, e["pend"][i][:, hh * hd:(hh + 1) * hd], 0.0),
                                          axis=1, keepdims=True), (c_len, hd)) for hh in range(nh)], axis=1)
            av = e["av"][i]
            store(i, kr=stack(e["kkt"][i], e["rt"][i]), tmat=e["tmat"][i].astype(BF16),
                  av=av[:c_len], arkv=av[c_len:2 * c_len], kv=av[2 * c_len:],
                  arbt=stack(e["a_rb"][i], _head_transpose(e["bvec"][i], nh)), pcat=pcat)

    return steps


def _scan_chain_steps(load, n_inst, states, emit):
    g = states[0].shape[1]
    nh = g // RWKV_HEAD
    c_len = RWKV_CHUNK
    bd = lambda y: _block_rows(y, nh, RWKV_HEAD)
    e = {}

    def s1():
        e["x"] = [_dot(load(i, "kr"), bd(states[i])) for i in range(n_inst)]

    def s2():
        e["u"] = [_dot(load(i, "tmat"), bd(e["x"][i][:c_len] + load(i, "av"))) for i in range(n_inst)]

    def s3():
        prod = [_dot(load(i, "arbt"), bd(e["u"][i])) for i in range(n_inst)]
        emit([e["x"][i][c_len:] + load(i, "arkv") - prod[i][:c_len] for i in range(n_inst)])
        for i in range(n_inst):
            states[i] = states[i] * load(i, "pcat") + load(i, "kv") - prod[i][c_len:]

    return [s1, s2, s3]


_SCAN_FIELDS = ("kr", "tmat", "av", "arkv", "arbt", "kv", "pcat")


def _scan_kernel(tiles_per_seq, r_ref, lw_ref, k_ref, v_ref, kk_ref, b_ref, y_ref, mt_ref, *bufs):
    ti = pl.program_id(1)
    tt, d = r_ref.shape[1], r_ref.shape[2]
    ng = d // GROUP_LANES
    nchunk = tt // RWKV_CHUNK
    ninst = nchunk * ng
    in_refs = (r_ref, lw_ref, k_ref, v_ref, kk_ref, b_ref)
    buf = dict(zip(_SCAN_FIELDS, bufs))
    wslot = (ti % 2) * ninst
    rslot = ((ti + 1) % 2) * ninst

    @pl.when(ti == 0)
    def _():
        mt_ref[...] = jnp.zeros(mt_ref.shape, F32)
        for name in _SCAN_FIELDS:
            ref = buf[name]
            ref[pl.ds(ninst, ninst)] = jnp.zeros((ninst,) + ref.shape[1:], ref.dtype)

    def store(i, **fields):
        for name, val in fields.items():
            buf[name][wslot + i] = val

    keep = jnp.where((ti + tiles_per_seq - 1) % tiles_per_seq == 0, 0.0, 1.0)
    states = [mt_ref[gi] * keep for gi in range(ng)]
    ys = []
    blocks, chain = [], []
    per_block = SCAN_CHUNKS_PER_STEP * ng
    for b0 in range(0, ninst, per_block):
        idx = range(b0, b0 + per_block)
        ins = [[ref[0, (i // ng) * RWKV_CHUNK:(i // ng + 1) * RWKV_CHUNK,
                    (i % ng) * GROUP_LANES:(i % ng + 1) * GROUP_LANES] for i in idx]
               for ref in in_refs]
        blocks.append(_scan_local_steps(ins, lambda i, _b0=b0, **f: store(_b0 + i, **f)))
    local = []
    for bi, steps in enumerate(blocks):
        at = min(len(local), bi * SCAN_BLOCK_STAGGER)
        merged = local[:at]
        rest = local[at:]
        for k in range(max(len(rest), len(steps))):
            merged += rest[k:k + 1] + steps[k:k + 1]
        local = merged
    for ci in range(nchunk):
        chain += _scan_chain_steps(lambda i, name, _c=ci: buf[name][rslot + _c * ng + i], ng,
                                   states, ys.append)
    done = 0
    span = max(1, (3 * len(local)) // 4)
    for j, f in enumerate(local):
        f()
        want = min(len(chain), -((j + 1) * len(chain) // -span))
        while done < want:
            chain[done]()
            done += 1
    for ci in range(nchunk):
        y_ref[0, ci * RWKV_CHUNK:(ci + 1) * RWKV_CHUNK, :] = jnp.concatenate(ys[ci], axis=-1)
    for gi in range(ng):
        mt_ref[gi] = states[gi]


def _rwkv_scan(r, lw, k, v, kk, b):
    bsz, t, d = r.shape
    tt = min(ROW_TILE, t)
    nt = bsz * (t // tt)
    ng = d // GROUP_LANES
    slots = 2 * (tt // RWKV_CHUNK) * ng
    c, g = RWKV_CHUNK, GROUP_LANES
    in_spec = pl.BlockSpec((1, tt, d), lambda bi, i: (bi, jnp.minimum(i, nt - 1), 0))
    out_spec = pl.BlockSpec((1, tt, d), lambda bi, i: (bi, jnp.maximum(i - 1, 0), 0))
    flat = lambda a: a.reshape(1, bsz * t, d)
    shapes = dict(kr=((2 * c, g), BF16), tmat=((c, g), BF16), av=((c, g), F32), arkv=((c, g), F32),
                  arbt=((c + RWKV_HEAD, g), BF16), kv=((RWKV_HEAD, g), F32), pcat=((RWKV_HEAD, g), F32))
    return pl.pallas_call(
        functools.partial(_scan_kernel, t // tt),
        grid=(1, nt + 1),
        in_specs=[in_spec] * 6,
        out_specs=out_spec,
        out_shape=jax.ShapeDtypeStruct((1, bsz * t, d), F32),
        scratch_shapes=[pltpu.VMEM((ng, RWKV_HEAD, g), F32)]
        + [pltpu.VMEM((slots,) + shapes[n][0], shapes[n][1]) for n in _SCAN_FIELDS],
        compiler_params=_params(),
        name="rwkv_scan",
    )(flat(r), flat(lw), flat(k), flat(v), flat(kk), flat(b)).reshape(bsz, t, d)


def _odd_post_kernel(res_ref, ada_ref, y_ref, r_ref, k_ref, v_ref, z_ref, rk_ref, lg_ref, lb_ref,
                     postg_ref, wout_ref, out_ref):
    d = res_ref.shape[2]
    tt = res_ref.shape[1]
    red, spread = _head_reduce_matrix(d), _head_spread_matrix(d)
    inv = 1.0 / RWKV_HEAD
    y = y_ref[0]
    rkr = r_ref[0] * k_ref[0] * rk_ref[...]
    sums = _dot(jnp.concatenate([y, rkr], axis=0), red)
    both = _dot(jnp.concatenate([sums[:tt] * inv, sums[tt:]], axis=0), spread)
    yc = y - both[:tt]
    rstd = lax.rsqrt(_dot(yc * yc, red) * inv + LNX_EPS)
    yn = yc * _dot(rstd, spread) * lg_ref[...] + lb_ref[...]
    out = _dot((yn + both[tt:] * v_ref[0]) * _silu(z_ref[0]), wout_ref[...])
    gate = ada_ref[0][:, 2 * d:]
    out_ref[0] = res_ref[0] + gate * _rms(out, postg_ref[...])


def _odd_post(res, ada, y, r, k, v, z, r_k, lnx_g, lnx_b, post_g, w_out):
    bsz, t, d = res.shape
    tt = min(WIDE_ROW_TILE, t)
    row_spec = pl.BlockSpec((1, tt, d), lambda b, i: (b, i, 0))
    vec = lambda a: a.reshape(1, d)
    return pl.pallas_call(
        _odd_post_kernel,
        grid=(bsz, t // tt),
        in_specs=[row_spec, pl.BlockSpec((1, 1, 3 * d), lambda b, i: (b, 0, 0))] + [row_spec] * 5
        + [_full_spec((1, d))] * 4 + [_full_spec(w_out.shape)],
        out_specs=row_spec,
        out_shape=jax.ShapeDtypeStruct(res.shape, F32),
        compiler_params=_params(),
        name="rwkv_post",
    )(res, ada.reshape(bsz, 1, 3 * d), y, r, k, v, z, vec(r_k), vec(lnx_g), vec(lnx_b), vec(post_g),
      w_out.astype(BF16))


def kernel(x, c, ada_w, ada_b, pre_g, post_g, ev_w_in, ev_w_out, pool_w, pool_scale, hgrn_lb_logits,
           hgrn_onorm_g, rw_mu, rw_w_rkvz, rw_w0, rw_w1, rw_w2, rw_a0, rw_a1, rw_a2, rw_k_k, rw_k_a,
           rw_r_k, rw_lnx_g, rw_lnx_b, rw_w_out, rw_v0, rw_v1, rw_v2):
    depth = ada_w.shape[0]
    res = x.astype(F32)
    ada = _ada_all(c, ada_w, ada_b)
    lb_all = _lower_bounds(hgrn_lb_logits)
    v_first = None
    for layer in range(depth):
        j = layer // 2
        if layer % 2 == 0:
            res = _even_layer(res, ada[layer], pre_g[layer], post_g[layer], ev_w_in[j], ev_w_out[j],
                              pool_w[j], pool_scale[j], lb_all[j], hgrn_onorm_g[j])
        else:
            vfirst = None if v_first is None else (v_first, rw_v0[j - 1], rw_v1[j - 1], rw_v2[j - 1])
            r, lw, k, v, kk, b, z = _odd_pre(res, ada[layer], pre_g[layer], rw_mu[j], rw_w_rkvz[j],
                                             rw_w0[j], rw_w1[j], rw_w2[j], rw_a0[j], rw_a1[j],
                                             rw_a2[j], rw_k_k[j], rw_k_a[j], vfirst)
            if v_first is None:
                v_first = v
            y = _rwkv_scan(r, lw, k, v, kk, b)
            res = _odd_post(res, ada[layer], y, r, k, v, z, rw_r_k[j], rw_lnx_g[j], rw_lnx_b[j],
                            post_g[layer], rw_w_out[j])
    return res.astype(x.dtype)
```
